```python
import jax, jax.numpy as jnp
from jax import lax
import numpy as np

D_MODEL = 1024
BATCH = 8
SEQ = 2048
DEPTH = 1

CONV_CH = D_MODEL
CONV_WIDTH = 31
HEAD_DIM = 64
N_Q_HEADS = D_MODEL // HEAD_DIM
N_KV_HEADS = N_Q_HEADS // 8
GROUP = N_Q_HEADS // N_KV_HEADS
WINDOW = 128
ROT_DIM = HEAD_DIM // 4
ROPE_THETA = 500000.0
W_CONV_COLS = 2 * CONV_CH
W_Q_COLS = N_Q_HEADS * HEAD_DIM
W_KV_COLS = N_KV_HEADS * HEAD_DIM
W_GATE_COLS = 2 * D_MODEL
IN_COLS = W_CONV_COLS + W_Q_COLS + 2 * W_KV_COLS + W_GATE_COLS
N_EXPERTS = 32
TOP_K = 4
D_EXPERT = D_MODEL
SWIGLU_LIMIT = 7.0
SWIGLU_ALPHA = 1.702
EXPERT_BLOCK = 128
RMS_EPS = 1e-5
LN_EPS = 1e-5

kernel_name = "hybrid_conformer_swa_sink_moe_block"

F32 = jnp.float32


def rms_norm(x, g):
    xf = x.astype(F32)
    y = xf * lax.rsqrt(jnp.mean(xf * xf, axis=-1, keepdims=True) + RMS_EPS)
    return (y * g.astype(F32)).astype(x.dtype)


def layer_norm(x, g, b):
    xf = x.astype(F32)
    mu = jnp.mean(xf, axis=-1, keepdims=True)
    var = jnp.mean(jnp.square(xf - mu), axis=-1, keepdims=True)
    y = (xf - mu) * lax.rsqrt(var + LN_EPS)
    return (y * g.astype(F32) + b.astype(F32)).astype(x.dtype)


def partial_rotary(x, positions):
    inv_freq = ROPE_THETA ** (-jnp.arange(0, ROT_DIM, 2, dtype=F32) / ROT_DIM)
    ang = positions.astype(F32)[..., None] * inv_freq
    cos = jnp.cos(ang)[:, :, None, :]
    sin = jnp.sin(ang)[:, :, None, :]
    xr = x[..., :ROT_DIM].astype(F32)
    x1, x2 = xr[..., : ROT_DIM // 2], xr[..., ROT_DIM // 2:]
    rot = jnp.concatenate([x1 * cos - x2 * sin, x2 * cos + x1 * sin], axis=-1)
    return jnp.concatenate([rot.astype(x.dtype), x[..., ROT_DIM:]], axis=-1)


def conformer_conv_branch(a, w_dw, b_dw, g_ln, b_ln, w_proj):
    val, gate = jnp.split(a, 2, axis=-1)
    c = val * jax.nn.sigmoid(gate)
    c = lax.conv_general_dilated(
        c, w_dw[:, None, :], window_strides=(1,),
        padding=[(CONV_WIDTH - 1, 0)],
        dimension_numbers=("NWC", "WIO", "NWC"),
        feature_group_count=CONV_CH) + b_dw
    c = layer_norm(c, g_ln, b_ln)
    c = jax.nn.silu(c)
    return c @ w_proj


def sliding_window_sink_attention(q, k, v, sinks):
    B, S = q.shape[0], q.shape[1]
    nb = S // WINDOW
    qb = q.reshape(B, nb, WINDOW, N_KV_HEADS, GROUP, HEAD_DIM)
    kb = k.reshape(B, nb, WINDOW, N_KV_HEADS, HEAD_DIM)
    vb = v.reshape(B, nb, WINDOW, N_KV_HEADS, HEAD_DIM)
    pad = jnp.zeros_like(kb[:, :1])
    kk = jnp.concatenate([jnp.concatenate([pad, kb[:, :-1]], axis=1), kb], axis=2)
    vv = jnp.concatenate([jnp.concatenate([pad, vb[:, :-1]], axis=1), vb], axis=2)
    scale = HEAD_DIM ** -0.5
    s = jnp.einsum("bnqhgd,bnkhd->bnhgqk", qb, kk, preferred_element_type=F32) * scale
    q_idx = jnp.arange(WINDOW)[:, None] + WINDOW
    k_idx = jnp.arange(2 * WINDOW)[None, :]
    rel = q_idx - k_idx
    band = (rel >= 0) & (rel < WINDOW)
    first = (jnp.arange(nb)[:, None, None] > 0) | (k_idx[None] >= WINDOW)
    mask = band[None] & first
    s = jnp.where(mask[None, :, None, None], s, -jnp.inf)
    sink = jnp.broadcast_to(sinks.astype(F32).reshape(1, 1, N_KV_HEADS, GROUP, 1, 1),
                            s.shape[:-1] + (1,))
    p = jax.nn.softmax(jnp.concatenate([s, sink], axis=-1), axis=-1)[..., :-1]
    o = jnp.einsum("bnhgqk,bnkhd->bnqhgd", p.astype(vv.dtype), vv)
    return o.reshape(B, S, N_Q_HEADS * HEAD_DIM)


def moe_ffn(u, w_router, b_router, w_mlp1, b_mlp1, w_mlp2, b_mlp2):
    B, S, D = u.shape
    t = u.reshape(-1, D)
    T = t.shape[0]
    n_assign = T * TOP_K
    logits = (t @ w_router + b_router).astype(F32)
    top_val, top_idx = lax.top_k(logits, TOP_K)
    gate = jax.nn.softmax(top_val, axis=-1)
    flat_e = top_idx.reshape(-1)
    flat_tok = jnp.arange(n_assign, dtype=jnp.int32) // TOP_K
    flat_g = gate.reshape(-1)
    order = jnp.argsort(flat_e)
    se = flat_e[order]
    counts = jnp.bincount(flat_e, length=N_EXPERTS)
    padded = (counts + EXPERT_BLOCK - 1) // EXPERT_BLOCK * EXPERT_BLOCK
    starts = jnp.cumsum(counts) - counts
    pstarts = jnp.cumsum(padded) - padded
    dest = pstarts[se] + jnp.arange(n_assign) - starts[se]
    n_blk = (n_assign + EXPERT_BLOCK - 1) // EXPERT_BLOCK + N_EXPERTS
    n_rows = n_blk * EXPERT_BLOCK
    row_tok = jnp.zeros((n_rows,), jnp.int32).at[dest].set(flat_tok[order])
    row_gate = jnp.zeros((n_rows,), F32).at[dest].set(flat_g[order])
    pends = jnp.cumsum(padded)
    blk_start = jnp.arange(n_blk) * EXPERT_BLOCK
    blk_e = jnp.minimum(jnp.sum(blk_start[:, None] >= pends[None, :], axis=1), N_EXPERTS - 1)
    xs = t[row_tok].reshape(n_blk, EXPERT_BLOCK, D)

    def expert_block(args):
        xb, e = args
        h = xb @ w_mlp1[e] + b_mlp1[e]
        glu, lin = jnp.split(h, 2, axis=-1)
        glu = jnp.minimum(glu, SWIGLU_LIMIT)
        lin = jnp.clip(lin, -SWIGLU_LIMIT, SWIGLU_LIMIT)
        a = glu * jax.nn.sigmoid(SWIGLU_ALPHA * glu) * (lin + 1)
        return a @ w_mlp2[e] + b_mlp2[e]

    ys = lax.map(expert_block, (xs, blk_e)).reshape(n_rows, D)
    out = jnp.zeros((T, D), u.dtype).at[row_tok].add(ys * row_gate[:, None].astype(ys.dtype))
    return out.reshape(B, S, D)


def setup_inputs(seed: int = 0) -> dict:
    key = jax.random.key(seed)
    ks = jax.random.split(key, 20)
    L, D, C, E, F = DEPTH, D_MODEL, CONV_CH, N_EXPERTS, D_EXPERT
    nrm = lambda k, shape, fan: jax.random.normal(k, shape, F32) * (fan ** -0.5)
    x = jax.random.normal(ks[0], (BATCH, SEQ, D), F32)
    positions = jnp.broadcast_to(jnp.arange(SEQ, dtype=jnp.int32), (BATCH, SEQ))
    return {
        "x": x,
        "positions": positions,
        "g_mix_norm": 1.0 + 0.02 * jax.random.normal(ks[1], (L, D), F32),
        "w_in": nrm(ks[2], (L, D, IN_COLS), D),
        "w_dw": nrm(ks[3], (L, CONV_WIDTH, C), CONV_WIDTH),
        "b_dw": 0.01 * jax.random.normal(ks[4], (L, C), F32),
        "g_conv_ln": 1.0 + 0.02 * jax.random.normal(ks[5], (L, C), F32),
        "b_conv_ln": 0.01 * jax.random.normal(ks[6], (L, C), F32),
        "w_conv_out": nrm(ks[7], (L, C, D), C),
        "attn_sinks": 0.5 * jax.random.normal(ks[8], (L, N_Q_HEADS), F32),
        "w_attn_out": nrm(ks[9], (L, N_Q_HEADS * HEAD_DIM, D), N_Q_HEADS * HEAD_DIM),
        "w_out": nrm(ks[10], (L, D, D), D),
        "g_ffn_norm": 1.0 + 0.02 * jax.random.normal(ks[11], (L, D), F32),
        "w_router": nrm(ks[12], (L, D, E), D),
        "b_router": 0.01 * jax.random.normal(ks[13], (L, E), F32),
        "w_mlp1": nrm(ks[14], (L, E, D, 2 * F), D),
        "b_mlp1": 0.01 * jax.random.normal(ks[15], (L, E, 2 * F), F32),
        "w_mlp2": nrm(ks[16], (L, E, F, D), F),
        "b_mlp2": 0.01 * jax.random.normal(ks[17], (L, E, D), F32),
        "g_final": 1.0 + 0.02 * jax.random.normal(ks[18], (D,), F32),
    }


def reference(x, positions, g_mix_norm, w_in, w_dw, b_dw, g_conv_ln, b_conv_ln, w_conv_out,
              attn_sinks, w_attn_out, w_out, g_ffn_norm, w_router, b_router,
              w_mlp1, b_mlp1, w_mlp2, b_mlp2, g_final):
    B, S, D = x.shape
    h = x
    split_at = np.cumsum([W_CONV_COLS, W_Q_COLS, W_KV_COLS, W_KV_COLS]).tolist()
    for l in range(DEPTH):
        u = rms_norm(h, g_mix_norm[l])
        proj = u @ w_in[l]
        a_conv, q, k, v, gates = jnp.split(proj, split_at, axis=-1)
        y_conv = conformer_conv_branch(a_conv, w_dw[l], b_dw[l], g_conv_ln[l],
                                       b_conv_ln[l], w_conv_out[l])
        q = partial_rotary(q.reshape(B, S, N_Q_HEADS, HEAD_DIM), positions)
        k = partial_rotary(k.reshape(B, S, N_KV_HEADS, HEAD_DIM), positions)
        v = v.reshape(B, S, N_KV_HEADS, HEAD_DIM)
        y_attn = sliding_window_sink_attention(q, k, v, attn_sinks[l]) @ w_attn_out[l]
        g_conv, g_attn = jnp.split(jax.nn.sigmoid(gates), 2, axis=-1)
        h = h + (g_conv * y_conv + g_attn * y_attn) @ w_out[l]
        u2 = rms_norm(h, g_ffn_norm[l])
        h = h + moe_ffn(u2, w_router[l], b_router[l], w_mlp1[l], b_mlp1[l],
                        w_mlp2[l], b_mlp2[l])
    return rms_norm(h, g_final)
```

```python
import functools

import jax
import jax.numpy as jnp
import numpy as np
from jax import lax
from jax.experimental import pallas as pl
from jax.experimental.pallas import tpu as pltpu

F32 = jnp.float32
BF16 = jnp.bfloat16

D_MODEL = 1024
CONV_WIDTH = 31
HEAD_DIM = 64
N_Q_HEADS = 16
N_KV_HEADS = 2
WINDOW = 128
ROT_DIM = 16
ROPE_THETA = 500000.0
N_EXPERTS = 32
TOP_K = 4
SWIGLU_LIMIT = 7.0
SWIGLU_ALPHA = 1.702
RMS_EPS = 1e-5
LN_EPS = 1e-5

LANES = 128
SUBLANES = 8
CHUNKS = D_MODEL // LANES

COL_VAL = 0
COL_GATE = D_MODEL
COL_Q = 2 * D_MODEL
COL_K = COL_Q + N_Q_HEADS * HEAD_DIM
COL_V = COL_K + N_KV_HEADS * HEAD_DIM
COL_G = COL_V + N_KV_HEADS * HEAD_DIM
IN_COLS = COL_G + 2 * D_MODEL

TQ = 256
CONV_HALO = 32
ROUTE_TOK = 2048
ROUTE_SUB = 512
BM = 256
DISPATCH_TOK = 512
COMBINE_TOK = 256
VMEM_LIMIT = 56 * 1024 * 1024


def _sigmoid(v):
    return 1.0 / (1.0 + jnp.exp(-v))


def _mixer_kernel(x_ref, pos_ref, gmix_ref, win_ref, wdw_ref, bdw_ref, gln_ref,
                  bln_ref, wco_ref, sinks_ref, wao_ref, wout_ref, gffn_ref, wrt_ref, brt_ref,
                  h_ref, u2_ref, topi_ref, topg_ref,
                  cbuf, kprev, vprev, conv_scr, o_scr):
    s = pl.program_id(1)

    @pl.when(s == 0)
    def _():
        cbuf[0:CONV_HALO, :] = jnp.zeros((CONV_HALO, D_MODEL), F32)
        kprev[...] = jnp.zeros_like(kprev)
        vprev[...] = jnp.zeros_like(vprev)

    x = x_ref[...]
    ms = jnp.mean(x * x, axis=-1, keepdims=True)
    u = (x * lax.rsqrt(ms + RMS_EPS) * gmix_ref[...]).astype(BF16)

    def proj(lo, hi):
        return jnp.dot(u, win_ref[:, lo:hi], preferred_element_type=F32)

    c = proj(COL_VAL, COL_GATE) * _sigmoid(proj(COL_GATE, COL_Q))
    cbuf[CONV_HALO:CONV_HALO + TQ, :] = c
    row_chunk = 64
    for lt in range(CHUNKS):
        lanes = slice(lt * LANES, (lt + 1) * LANES)
        for rc in range(TQ // row_chunk):
            base = rc * row_chunk + CONV_HALO - (CONV_WIDTH - 1)
            acc = jnp.zeros((row_chunk, LANES), F32)
            for j in range(CONV_WIDTH):
                acc = acc + wdw_ref[j:j + 1, lanes] * cbuf[base + j:base + j + row_chunk, lanes]
            conv_scr[rc * row_chunk:(rc + 1) * row_chunk, lanes] = acc
    cbuf[0:CONV_HALO, :] = cbuf[TQ:TQ + CONV_HALO, :]
    cv = conv_scr[...] + bdw_ref[...]
    mu = jnp.mean(cv, axis=-1, keepdims=True)
    cc = cv - mu
    var = jnp.mean(cc * cc, axis=-1, keepdims=True)
    cn = cc * lax.rsqrt(var + LN_EPS) * gln_ref[...] + bln_ref[...]
    cn = cn * _sigmoid(cn)
    y_conv = jnp.dot(cn.astype(BF16), wco_ref[...], preferred_element_type=F32)

    dim = lax.broadcasted_iota(jnp.int32, (1, LANES), 1) % HEAD_DIM
    freq = ((dim % (ROT_DIM // 2)) * 2).astype(F32)
    inv_freq = jnp.where(dim < ROT_DIM, jnp.power(jnp.float32(ROPE_THETA), -freq / ROT_DIM), 0.0)
    ang = pos_ref[...].astype(F32) * inv_freq
    cos = jnp.cos(ang)
    sin = jnp.sin(ang)
    sin_lo = jnp.where(dim < ROT_DIM // 2, -sin, 0.0)
    sin_hi = jnp.where((dim >= ROT_DIM // 2) & (dim < ROT_DIM), sin, 0.0)

    def rotary(t):
        up = pltpu.roll(t, LANES - ROT_DIM // 2, 1)
        dn = pltpu.roll(t, ROT_DIM // 2, 1)
        return t * cos + up * sin_lo + dn * sin_hi

    kv = proj(COL_K, COL_G)
    k_rot = rotary(kv[:, :LANES]).astype(BF16)
    v_cur = kv[:, LANES:].astype(BF16)
    q = proj(COL_Q, COL_K) * (HEAD_DIM ** -0.5)
    q_rot = [rotary(q[:, p * LANES:(p + 1) * LANES]).astype(BF16) for p in range(CHUNKS)]

    lane = lax.broadcasted_iota(jnp.int32, (1, LANES), 1)
    lo_half = lane < HEAD_DIM
    n_stack = N_Q_HEADS // N_KV_HEADS // 2
    qi = lax.broadcasted_iota(jnp.int32, (n_stack * WINDOW, 2 * WINDOW), 0) % WINDOW
    ki = lax.broadcasted_iota(jnp.int32, (n_stack * WINDOW, 2 * WINDOW), 1)
    band = (ki > qi) & (ki <= qi + WINDOW)

    for qb in range(TQ // WINDOW):
        rows = slice(qb * WINDOW, (qb + 1) * WINDOW)
        if qb == 0:
            k_cat = jnp.concatenate([kprev[...], k_rot[rows]], axis=0)
            v_cat = jnp.concatenate([vprev[...], v_cur[rows]], axis=0)
            mask = band & (ki >= jnp.where(s > 0, 0, WINDOW))
        else:
            prev = slice((qb - 1) * WINDOW, qb * WINDOW)
            k_cat = jnp.concatenate([k_rot[prev], k_rot[rows]], axis=0)
            v_cat = jnp.concatenate([v_cur[prev], v_cur[rows]], axis=0)
            mask = band
        k_sw = jnp.concatenate([k_cat[:, HEAD_DIM:], k_cat[:, :HEAD_DIM]], axis=1)
        v_sw = jnp.concatenate([v_cat[:, HEAD_DIM:], v_cat[:, :HEAD_DIM]], axis=1)
        zero = jnp.zeros_like(k_cat)
        for g in range(N_KV_HEADS):
            k_src, v_src = (k_cat, v_cat) if g == 0 else (k_sw, v_sw)
            k_alt, v_alt = (k_sw, v_sw) if g == 0 else (k_cat, v_cat)
            k_lo = jnp.where(lo_half, k_src, zero)
            v_lo = jnp.where(lo_half, v_src, zero)
            k_hi = jnp.where(lo_half, zero, k_alt)
            v_hi = jnp.where(lo_half, zero, v_alt)
            q_stack = jnp.concatenate(
                [q_rot[g * n_stack + i][rows] for i in range(n_stack)], axis=0)
            probs = []
            for parity, k_pad in ((0, k_lo), (1, k_hi)):
                sc = lax.dot_general(q_stack, k_pad, (((1,), (1,)), ((), ())),
                                     preferred_element_type=F32)
                sc = jnp.where(mask, sc, -jnp.inf)
                sink = jnp.concatenate(
                    [jnp.full((WINDOW, 1), sinks_ref[2 * (g * n_stack + i) + parity], F32)
                     for i in range(n_stack)], axis=0)
                m = jnp.maximum(jnp.max(sc, axis=-1, keepdims=True), sink)
                e = jnp.exp(sc - m)
                den = jnp.sum(e, axis=-1, keepdims=True) + jnp.exp(sink - m)
                probs.append((e / den).astype(BF16))
            o_stack = (jnp.dot(probs[0], v_lo, preferred_element_type=F32)
                       + jnp.dot(probs[1], v_hi, preferred_element_type=F32))
            for i in range(n_stack):
                p = g * n_stack + i
                o_scr[rows, p * LANES:(p + 1) * LANES] = (
                    o_stack[i * WINDOW:(i + 1) * WINDOW].astype(BF16))
    last = slice(TQ - WINDOW, TQ)
    kprev[...] = k_rot[last]
    vprev[...] = v_cur[last]
    y_attn = jnp.dot(o_scr[...], wao_ref[...], preferred_element_type=F32)

    gates = _sigmoid(proj(COL_G, IN_COLS))
    mix = gates[:, :D_MODEL] * y_conv + gates[:, D_MODEL:] * y_attn
    h = x + jnp.dot(mix.astype(BF16), wout_ref[...], preferred_element_type=F32)
    h_ref[...] = h

    ms2 = jnp.mean(h * h, axis=-1, keepdims=True)
    u2 = h * lax.rsqrt(ms2 + RMS_EPS) * gffn_ref[...]
    for cidx in range(CHUNKS):
        u2_ref[pl.ds(cidx, TQ, stride=CHUNKS), :] = u2[:, cidx * LANES:(cidx + 1) * LANES]
    logits = lax.dot_general(wrt_ref[...], u2, (((1,), (1,)), ((), ())),
                             precision=lax.Precision.HIGHEST,
                             preferred_element_type=F32) + brt_ref[...]
    eidx = lax.broadcasted_iota(jnp.int32, (N_EXPERTS, TQ), 0)
    vals, idxs = [], []
    for _ in range(TOP_K):
        m = jnp.max(logits, axis=0, keepdims=True)
        sel = jnp.min(jnp.where(logits == m, eidx, N_EXPERTS), axis=0, keepdims=True)
        vals.append(m)
        idxs.append(sel)
        logits = jnp.where(eidx == sel, -jnp.inf, logits)
    ex = [jnp.exp(v - vals[0]) for v in vals]
    den = ex[0] + ex[1] + ex[2] + ex[3]
    topi_ref[...] = jnp.concatenate(idxs, axis=0)
    topg_ref[...] = jnp.concatenate([e / den for e in ex], axis=0)


def _const_spec(shape):
    return pl.BlockSpec(shape, lambda b, s: (0,) * len(shape))


def _mixer(x, pos3, gmix, win, wdw, bdw, gln, bln, wco, sinks, wao, wout, gffn, wrt, brt):
    B, S, D = x.shape
    T = B * S
    n_s = S // TQ
    row_map = lambda b, s: (b * n_s + s, 0)
    return pl.pallas_call(
        _mixer_kernel,
        grid=(B, n_s),
        in_specs=[
            pl.BlockSpec((None, TQ, D), lambda b, s: (b, s, 0)),
            pl.BlockSpec((None, TQ, 1), lambda b, s: (b, s, 0)),
            _const_spec((1, D)),
            _const_spec((D, IN_COLS)),
            _const_spec((CONV_WIDTH, D)),
            _const_spec((1, D)),
            _const_spec((1, D)),
            _const_spec((1, D)),
            _const_spec((D, D)),
            pl.BlockSpec(memory_space=pltpu.SMEM),
            _const_spec((D, D)),
            _const_spec((D, D)),
            _const_spec((1, D)),
            _const_spec((N_EXPERTS, D)),
            _const_spec((N_EXPERTS, 1)),
        ],
        out_specs=[
            pl.BlockSpec((TQ, D), row_map),
            pl.BlockSpec((TQ * CHUNKS, LANES), row_map),
            pl.BlockSpec((TOP_K, TQ), lambda b, s: (0, b * n_s + s)),
            pl.BlockSpec((TOP_K, TQ), lambda b, s: (0, b * n_s + s)),
        ],
        out_shape=[
            jax.ShapeDtypeStruct((T, D), F32),
            jax.ShapeDtypeStruct((T * CHUNKS, LANES), F32),
            jax.ShapeDtypeStruct((TOP_K, T), jnp.int32),
            jax.ShapeDtypeStruct((TOP_K, T), F32),
        ],
        scratch_shapes=[
            pltpu.VMEM((TQ + CONV_HALO, D), F32),
            pltpu.VMEM((WINDOW, LANES), BF16),
            pltpu.VMEM((WINDOW, LANES), BF16),
            pltpu.VMEM((TQ, D), F32),
            pltpu.VMEM((TQ, D), BF16),
        ],
        compiler_params=pltpu.CompilerParams(
            dimension_semantics=("arbitrary", "arbitrary"),
            vmem_limit_bytes=VMEM_LIMIT),
        name="mixer",
    )(x, pos3, gmix, win, wdw, bdw, gln, bln, wco, sinks, wao, wout, gffn, wrt, brt)


def _route_kernel(topi_ref, dest_ref, blk_ref, ends_ref, cnt, base, *, n_blk_pad):
    phase = pl.program_id(0)
    step = pl.program_id(1)
    n_sub = ROUTE_TOK // ROUTE_SUB
    eidx = lax.broadcasted_iota(jnp.int32, (N_EXPERTS, ROUTE_SUB), 0)

    def onehot(j, sub):
        idx = topi_ref[j:j + 1, sub * ROUTE_SUB:(sub + 1) * ROUTE_SUB]
        return eidx == idx

    @pl.when((phase == 0) & (step == 0))
    def _():
        cnt[...] = jnp.zeros_like(cnt)

    @pl.when(phase == 0)
    def _():
        tot = jnp.zeros((N_EXPERTS, 1), F32)
        for sub in range(n_sub):
            for j in range(TOP_K):
                tot = tot + jnp.sum(onehot(j, sub).astype(F32), axis=1, keepdims=True)
        cnt[...] = cnt[...] + tot

    @pl.when((phase == 1) & (step == 0))
    def _():
        counts = cnt[...]
        padded = jnp.floor((counts + (BM - 1)) * (1.0 / BM)) * BM
        sub_i = lax.broadcasted_iota(jnp.int32, (N_EXPERTS, LANES), 0)
        lane_i = lax.broadcasted_iota(jnp.int32, (N_EXPERTS, LANES), 1)
        row = jnp.sum(jnp.where(sub_i == lane_i, padded, 0.0), axis=0, keepdims=True)
        ends = jnp.sum(jnp.where(lane_i <= sub_i, row, 0.0), axis=1, keepdims=True)
        base[...] = jnp.broadcast_to(ends, (N_EXPERTS, LANES)) - padded
        ends_row = jnp.sum(jnp.where(sub_i == lane_i, ends, 0.0), axis=0, keepdims=True)
        ends_ref[...] = ends_row.astype(jnp.int32)
        blk_start = (lax.broadcasted_iota(jnp.int32, (N_EXPERTS, n_blk_pad), 1) * BM).astype(F32)
        blk = jnp.sum((blk_start >= ends).astype(jnp.int32), axis=0, keepdims=True)
        blk_ref[...] = jnp.minimum(blk, N_EXPERTS - 1)

    @pl.when(phase == 1)
    def _():
        r = lax.broadcasted_iota(jnp.int32, (ROUTE_SUB, ROUTE_SUB), 0)
        cidx = lax.broadcasted_iota(jnp.int32, (ROUTE_SUB, ROUTE_SUB), 1)
        upper = jnp.where(r < cidx, 1.0, 0.0).astype(BF16)
        run = base[...][:, 0:1]
        for sub in range(n_sub):
            for j in range(TOP_K):
                oh = onehot(j, sub)
                ohf = oh.astype(F32)
                pre = jnp.dot(ohf.astype(BF16), upper, preferred_element_type=F32)
                rank = jnp.sum(jnp.where(oh, pre + run, 0.0), axis=0, keepdims=True)
                dest_ref[j:j + 1, sub * ROUTE_SUB:(sub + 1) * ROUTE_SUB] = rank.astype(jnp.int32)
                run = run + jnp.sum(ohf, axis=1, keepdims=True)
        base[...] = jnp.broadcast_to(run, (N_EXPERTS, LANES))


def _route(topi, n_blk_pad):
    T = topi.shape[1]
    n_steps = T // ROUTE_TOK
    return pl.pallas_call(
        functools.partial(_route_kernel, n_blk_pad=n_blk_pad),
        grid=(2, n_steps),
        in_specs=[pl.BlockSpec((TOP_K, ROUTE_TOK), lambda p, c: (0, c))],
        out_specs=[
            pl.BlockSpec((TOP_K, ROUTE_TOK), lambda p, c: (0, c * p)),
            pl.BlockSpec((1, n_blk_pad), lambda p, c: (0, 0)),
            pl.BlockSpec((1, LANES), lambda p, c: (0, 0)),
        ],
        out_shape=[
            jax.ShapeDtypeStruct((TOP_K, T), jnp.int32),
            jax.ShapeDtypeStruct((1, n_blk_pad), jnp.int32),
            jax.ShapeDtypeStruct((1, LANES), jnp.int32),
        ],
        scratch_shapes=[
            pltpu.VMEM((N_EXPERTS, LANES), F32),
            pltpu.VMEM((N_EXPERTS, LANES), F32),
        ],
        compiler_params=pltpu.CompilerParams(
            dimension_semantics=("arbitrary", "arbitrary")),
        name="route",
    )(topi)


def _slab(ref, row):
    return ref.at[pl.ds(pl.multiple_of(row * CHUNKS, CHUNKS), CHUNKS), :]


def _dispatch_kernel(dest_ref, ends_ref, u2_hbm, xs_hbm, zbuf, zsem, sem):
    step = pl.program_id(0)

    @pl.when(step == 0)
    def _():
        zbuf[...] = jnp.zeros_like(zbuf)

        def zero_copy(e):
            start = pl.multiple_of((ends_ref[0, e] - BM) * CHUNKS, BM * CHUNKS)
            return pltpu.make_async_copy(zbuf, xs_hbm.at[pl.ds(start, BM * CHUNKS), :], zsem)

        def nonempty(e):
            prev_end = jnp.where(e > 0, ends_ref[0, jnp.maximum(e - 1, 0)], 0)
            return ends_ref[0, e] > prev_end

        def start_body(e, carry):
            @pl.when(nonempty(e))
            def _():
                zero_copy(e).start()
            return carry

        def wait_body(e, carry):
            @pl.when(nonempty(e))
            def _():
                zero_copy(e).wait()
            return carry

        lax.fori_loop(0, N_EXPERTS, start_body, 0)
        lax.fori_loop(0, N_EXPERTS, wait_body, 0)

        def tail_copy(b):
            start = pl.multiple_of(b * (BM * CHUNKS), BM * CHUNKS)
            return pltpu.make_async_copy(zbuf, xs_hbm.at[pl.ds(start, BM * CHUNKS), :], zsem)

        def tail_start(b, carry):
            tail_copy(b).start()
            return carry

        def tail_wait(b, carry):
            tail_copy(b).wait()
            return carry

        n_used = ends_ref[0, N_EXPERTS - 1] // BM
        n_blk = xs_hbm.shape[0] // (BM * CHUNKS)
        lax.fori_loop(n_used, n_blk, tail_start, 0)
        lax.fori_loop(n_used, n_blk, tail_wait, 0)

    tok0 = step * DISPATCH_TOK

    def row_copy(t, j):
        return pltpu.make_async_copy(_slab(u2_hbm, tok0 + t), _slab(xs_hbm, dest_ref[j, t]), sem)

    def start_body(t, carry):
        for j in range(TOP_K):
            row_copy(t, j).start()
        return carry

    def wait_body(t, carry):
        for j in range(TOP_K):
            row_copy(t, j).wait()
        return carry

    lax.fori_loop(0, DISPATCH_TOK, start_body, 0)
    lax.fori_loop(0, DISPATCH_TOK, wait_body, 0)


def _dispatch(dest, ends, u2s, n_rows):
    T = dest.shape[1]
    return pl.pallas_call(
        _dispatch_kernel,
        grid=(T // DISPATCH_TOK,),
        in_specs=[
            pl.BlockSpec((TOP_K, DISPATCH_TOK), lambda i: (0, i), memory_space=pltpu.SMEM),
            pl.BlockSpec(memory_space=pltpu.SMEM),
            pl.BlockSpec(memory_space=pl.ANY),
        ],
        out_specs=pl.BlockSpec(memory_space=pl.ANY),
        out_shape=jax.ShapeDtypeStruct((n_rows * CHUNKS, LANES), F32),
        scratch_shapes=[
            pltpu.VMEM((BM * CHUNKS, LANES), F32),
            pltpu.SemaphoreType.DMA,
            pltpu.SemaphoreType.DMA,
        ],
        compiler_params=pltpu.CompilerParams(dimension_semantics=("arbitrary",)),
        name="dispatch",
    )(dest, ends, u2s)


def _expert_kernel(blk_ref, ends_ref, xs_ref, w1_ref, b1_ref, w2_ref, b2_ref, ys_ref):
    i = pl.program_id(0)
    n_used = ends_ref[0, N_EXPERTS - 1] // BM

    @pl.when(i < n_used)
    def _():
        xb = jnp.concatenate(
            [xs_ref[pl.ds(cidx, BM, stride=CHUNKS), :] for cidx in range(CHUNKS)],
            axis=1).astype(BF16)
        h1 = jnp.dot(xb, w1_ref[...], preferred_element_type=F32) + b1_ref[...]
        f = h1.shape[1] // 2
        glu = jnp.minimum(h1[:, :f], SWIGLU_LIMIT)
        lin = jnp.clip(h1[:, f:], -SWIGLU_LIMIT, SWIGLU_LIMIT)
        act = glu * _sigmoid(SWIGLU_ALPHA * glu) * (lin + 1.0)
        y = jnp.dot(act.astype(BF16), w2_ref[...], preferred_element_type=F32) + b2_ref[...]
        for cidx in range(CHUNKS):
            ys_ref[pl.ds(cidx, BM, stride=CHUNKS), :] = y[:, cidx * LANES:(cidx + 1) * LANES]

    @pl.when(i >= n_used)
    def _():
        ys_ref[...] = jnp.zeros_like(ys_ref)


def _experts(blk_e, ends, xs, w1, b1, w2, b2, n_blk):
    E, D, F2 = w1.shape

    def used(i, ends_ref):
        return jnp.minimum(i, ends_ref[0, N_EXPERTS - 1] // BM - 1)

    grid_spec = pltpu.PrefetchScalarGridSpec(
        num_scalar_prefetch=2,
        grid=(n_blk,),
        in_specs=[
            pl.BlockSpec((BM * CHUNKS, LANES), lambda i, blk, ends: (used(i, ends), 0)),
            pl.BlockSpec((None, D, F2), lambda i, blk, ends: (blk[0, used(i, ends)], 0, 0)),
            pl.BlockSpec((None, 1, F2), lambda i, blk, ends: (blk[0, used(i, ends)], 0, 0)),
            pl.BlockSpec((None, F2 // 2, D), lambda i, blk, ends: (blk[0, used(i, ends)], 0, 0)),
            pl.BlockSpec((None, 1, D), lambda i, blk, ends: (blk[0, used(i, ends)], 0, 0)),
        ],
        out_specs=pl.BlockSpec((BM * CHUNKS, LANES), lambda i, blk, ends: (i, 0)),
    )
    return pl.pallas_call(
        _expert_kernel,
        grid_spec=grid_spec,
        out_shape=jax.ShapeDtypeStruct((n_blk * BM * CHUNKS, LANES), F32),
        compiler_params=pltpu.CompilerParams(
            dimension_semantics=("arbitrary",), vmem_limit_bytes=VMEM_LIMIT),
        name="experts",
    )(blk_e, ends, xs, w1, b1, w2, b2)


def _combine_kernel(dest_ref, gate_ref, h_ref, gfin_ref, ys_hbm, out_ref, gbuf, sem):
    def row_copy(t, j):
        return pltpu.make_async_copy(
            _slab(ys_hbm, dest_ref[j, t]), _slab(gbuf, j * COMBINE_TOK + t), sem)

    def start_body(t, carry):
        for j in range(TOP_K):
            row_copy(t, j).start()
        return carry

    def wait_body(t, carry):
        for j in range(TOP_K):
            row_copy(t, j).wait()
        return carry

    lax.fori_loop(0, COMBINE_TOK, start_body, 0)
    lax.fori_loop(0, COMBINE_TOK, wait_body, 0)

    gate = gate_ref[...]
    parts = []
    ss = jnp.zeros((COMBINE_TOK, 1), F32)
    for cidx in range(CHUNKS):
        acc = h_ref[:, cidx * LANES:(cidx + 1) * LANES]
        for j in range(TOP_K):
            rows = gbuf[pl.ds(j * COMBINE_TOK * CHUNKS + cidx, COMBINE_TOK, stride=CHUNKS), :]
            acc = acc + gate[:, j:j + 1] * rows
        parts.append(acc)
        ss = ss + jnp.sum(acc * acc, axis=-1, keepdims=True)
    scale = lax.rsqrt(ss * (1.0 / D_MODEL) + RMS_EPS)
    for cidx in range(CHUNKS):
        lanes = slice(cidx * LANES, (cidx + 1) * LANES)
        out_ref[:, lanes] = parts[cidx] * scale * gfin_ref[:, lanes]


def _combine(dest, gate_t, h, gfin, ys):
    T, D = h.shape
    return pl.pallas_call(
        _combine_kernel,
        grid=(T // COMBINE_TOK,),
        in_specs=[
            pl.BlockSpec((TOP_K, COMBINE_TOK), lambda i: (0, i), memory_space=pltpu.SMEM),
            pl.BlockSpec((COMBINE_TOK, TOP_K), lambda i: (i, 0)),
            pl.BlockSpec((COMBINE_TOK, D), lambda i: (i, 0)),
            pl.BlockSpec((1, D), lambda i: (0, 0)),
            pl.BlockSpec(memory_space=pl.ANY),
        ],
        out_specs=pl.BlockSpec((COMBINE_TOK, D), lambda i: (i, 0)),
        out_shape=jax.ShapeDtypeStruct((T, D), F32),
        scratch_shapes=[
            pltpu.VMEM((TOP_K * COMBINE_TOK * CHUNKS, LANES), F32),
            pltpu.SemaphoreType.DMA,
        ],
        compiler_params=pltpu.CompilerParams(dimension_semantics=("arbitrary",)),
        name="combine",
    )(dest, gate_t, h, gfin, ys)


def kernel(x, positions, g_mix_norm, w_in, w_dw, b_dw, g_conv_ln, b_conv_ln, w_conv_out,
           attn_sinks, w_attn_out, w_out, g_ffn_norm, w_router, b_router,
           w_mlp1, b_mlp1, w_mlp2, b_mlp2, g_final):
    B, S, D = x.shape
    T = B * S
    depth = g_mix_norm.shape[0]
    assert depth == 1 and D == D_MODEL and S % TQ == 0 and T % ROUTE_TOK == 0
    l = 0
    n_blk = (T * TOP_K) // BM + N_EXPERTS
    n_blk_pad = -(-n_blk // LANES) * LANES

    h, u2s, topi, topg = _mixer(
        x, positions[..., None], g_mix_norm[l][None, :],
        w_in[l].astype(BF16), w_dw[l], b_dw[l][None, :], g_conv_ln[l][None, :],
        b_conv_ln[l][None, :], w_conv_out[l].astype(BF16), attn_sinks[l],
        w_attn_out[l].astype(BF16), w_out[l].astype(BF16), g_ffn_norm[l][None, :],
        w_router[l].T, b_router[l][:, None])
    dest, blk_e, ends = _route(topi, n_blk_pad)
    xs = _dispatch(dest, ends, u2s, n_blk * BM)
    ys = _experts(blk_e, ends, xs, w_mlp1[l].astype(BF16), b_mlp1[l][:, None, :],
                  w_mlp2[l].astype(BF16), b_mlp2[l][:, None, :], n_blk)
    out = _combine(dest, topg.T, h, g_final[None, :], ys)
    return out.reshape(B, S, D)
```

```python
import functools

import jax
import jax.numpy as jnp
import numpy as np
from jax import lax
from jax.experimental import pallas as pl
from jax.experimental.pallas import tpu as pltpu

F32 = jnp.float32
BF16 = jnp.bfloat16

D_MODEL = 1024
CONV_WIDTH = 31
HEAD_DIM = 64
N_Q_HEADS = 16
N_KV_HEADS = 2
WINDOW = 128
ROT_DIM = 16
ROPE_THETA = 500000.0
N_EXPERTS = 32
TOP_K = 4
SWIGLU_LIMIT = 7.0
SWIGLU_ALPHA = 1.702
RMS_EPS = 1e-5
LN_EPS = 1e-5

LANES = 128
SUBLANES = 8
CHUNKS = D_MODEL // LANES

COL_VAL = 0
COL_GATE = D_MODEL
COL_Q = 2 * D_MODEL
COL_K = COL_Q + N_Q_HEADS * HEAD_DIM
COL_V = COL_K + N_KV_HEADS * HEAD_DIM
COL_G = COL_V + N_KV_HEADS * HEAD_DIM
IN_COLS = COL_G + 2 * D_MODEL

TQ = 256
CONV_HALO = 32
ROUTE_TOK = 2048
ROUTE_SUB = 512
BM = 256
DISPATCH_TOK = 512
COMBINE_TOK = 256
VMEM_LIMIT = 56 * 1024 * 1024


def _sigmoid(v):
    return 1.0 / (1.0 + jnp.exp(-v))


def _mixer_kernel(x_ref, pos_ref, gmix_ref, win_ref, wdw_ref, bdw_ref, gln_ref,
                  bln_ref, wco_ref, sinks_ref, wao_ref, wout_ref, gffn_ref, wrt_ref, brt_ref,
                  h_ref, u2_ref, topi_ref, topg_ref,
                  cbuf, kprev, vprev, conv_scr, o_scr):
    s = pl.program_id(1)

    @pl.when(s == 0)
    def _():
        cbuf[0:CONV_HALO, :] = jnp.zeros((CONV_HALO, D_MODEL), F32)
        kprev[...] = jnp.zeros_like(kprev)
        vprev[...] = jnp.zeros_like(vprev)

    x = x_ref[...]
    ms = jnp.mean(x * x, axis=-1, keepdims=True)
    u = (x * lax.rsqrt(ms + RMS_EPS) * gmix_ref[...]).astype(BF16)

    def proj(lo, hi):
        return jnp.dot(u, win_ref[:, lo:hi], preferred_element_type=F32)

    c = proj(COL_VAL, COL_GATE) * _sigmoid(proj(COL_GATE, COL_Q))
    cbuf[CONV_HALO:CONV_HALO + TQ, :] = c
    row_chunk = 64
    for lt in range(CHUNKS):
        lanes = slice(lt * LANES, (lt + 1) * LANES)
        for rc in range(TQ // row_chunk):
            base = rc * row_chunk + CONV_HALO - (CONV_WIDTH - 1)
            acc = jnp.zeros((row_chunk, LANES), F32)
            for j in range(CONV_WIDTH):
                acc = acc + wdw_ref[j:j + 1, lanes] * cbuf[base + j:base + j + row_chunk, lanes]
            conv_scr[rc * row_chunk:(rc + 1) * row_chunk, lanes] = acc
    cbuf[0:CONV_HALO, :] = cbuf[TQ:TQ + CONV_HALO, :]
    cv = conv_scr[...] + bdw_ref[...]
    mu = jnp.mean(cv, axis=-1, keepdims=True)
    cc = cv - mu
    var = jnp.mean(cc * cc, axis=-1, keepdims=True)
    cn = cc * lax.rsqrt(var + LN_EPS) * gln_ref[...] + bln_ref[...]
    cn = cn * _sigmoid(cn)
    y_conv = jnp.dot(cn.astype(BF16), wco_ref[...], preferred_element_type=F32)

    dim = lax.broadcasted_iota(jnp.int32, (1, LANES), 1) % HEAD_DIM
    freq = ((dim % (ROT_DIM // 2)) * 2).astype(F32)
    inv_freq = jnp.where(dim < ROT_DIM, jnp.power(jnp.float32(ROPE_THETA), -freq / ROT_DIM), 0.0)
    ang = pos_ref[...].astype(F32) * inv_freq
    cos = jnp.cos(ang)
    sin = jnp.sin(ang)
    sin_lo = jnp.where(dim < ROT_DIM // 2, -sin, 0.0)
    sin_hi = jnp.where((dim >= ROT_DIM // 2) & (dim < ROT_DIM), sin, 0.0)

    def rotary(t):
        up = pltpu.roll(t, LANES - ROT_DIM // 2, 1)
        dn = pltpu.roll(t, ROT_DIM // 2, 1)
        return t * cos + up * sin_lo + dn * sin_hi

    kv = proj(COL_K, COL_G)
    k_rot = rotary(kv[:, :LANES]).astype(BF16)
    v_cur = kv[:, LANES:].astype(BF16)
    q = proj(COL_Q, COL_K) * (HEAD_DIM ** -0.5)
    q_rot = [rotary(q[:, p * LANES:(p + 1) * LANES]).astype(BF16) for p in range(CHUNKS)]

    lane = lax.broadcasted_iota(jnp.int32, (1, LANES), 1)
    lo_half = lane < HEAD_DIM
    n_stack = N_Q_HEADS // N_KV_HEADS // 2
    qi = lax.broadcasted_iota(jnp.int32, (n_stack * WINDOW, 2 * WINDOW), 0) % WINDOW
    ki = lax.broadcasted_iota(jnp.int32, (n_stack * WINDOW, 2 * WINDOW), 1)
    band = (ki > qi) & (ki <= qi + WINDOW)

    for qb in range(TQ // WINDOW):
        rows = slice(qb * WINDOW, (qb + 1) * WINDOW)
        if qb == 0:
            k_cat = jnp.concatenate([kprev[...], k_rot[rows]], axis=0)
            v_cat = jnp.concatenate([vprev[...], v_cur[rows]], axis=0)
            mask = band & (ki >= jnp.where(s > 0, 0, WINDOW))
        else:
            prev = slice((qb - 1) * WINDOW, qb * WINDOW)
            k_cat = jnp.concatenate([k_rot[prev], k_rot[rows]], axis=0)
            v_cat = jnp.concatenate([v_cur[prev], v_cur[rows]], axis=0)
            mask = band
        k_sw = jnp.concatenate([k_cat[:, HEAD_DIM:], k_cat[:, :HEAD_DIM]], axis=1)
        v_sw = jnp.concatenate([v_cat[:, HEAD_DIM:], v_cat[:, :HEAD_DIM]], axis=1)
        zero = jnp.zeros_like(k_cat)
        for g in range(N_KV_HEADS):
            k_src, v_src = (k_cat, v_cat) if g == 0 else (k_sw, v_sw)
            k_alt, v_alt = (k_sw, v_sw) if g == 0 else (k_cat, v_cat)
            k_lo = jnp.where(lo_half, k_src, zero)
            v_lo = jnp.where(lo_half, v_src, zero)
            k_hi = jnp.where(lo_half, zero, k_alt)
            v_hi = jnp.where(lo_half, zero, v_alt)
            q_stack = jnp.concatenate(
                [q_rot[g * n_stack + i][rows] for i in range(n_stack)], axis=0)
            probs = []
            for parity, k_pad in ((0, k_lo), (1, k_hi)):
                sc = lax.dot_general(q_stack, k_pad, (((1,), (1,)), ((), ())),
                                     preferred_element_type=F32)
                sc = jnp.where(mask, sc, -jnp.inf)
                sink = jnp.concatenate(
                    [jnp.full((WINDOW, 1), sinks_ref[2 * (g * n_stack + i) + parity], F32)
                     for i in range(n_stack)], axis=0)
                m = jnp.maximum(jnp.max(sc, axis=-1, keepdims=True), sink)
                e = jnp.exp(sc - m)
                den = jnp.sum(e, axis=-1, keepdims=True) + jnp.exp(sink - m)
                probs.append((e / den).astype(BF16))
            o_stack = (jnp.dot(probs[0], v_lo, preferred_element_type=F32)
                       + jnp.dot(probs[1], v_hi, preferred_element_type=F32))
            for i in range(n_stack):
                p = g * n_stack + i
                o_scr[rows, p * LANES:(p + 1) * LANES] = (
                    o_stack[i * WINDOW:(i + 1) * WINDOW].astype(BF16))
    last = slice(TQ - WINDOW, TQ)
    kprev[...] = k_rot[last]
    vprev[...] = v_cur[last]
    y_attn = jnp.dot(o_scr[...], wao_ref[...], preferred_element_type=F32)

    gates = _sigmoid(proj(COL_G, IN_COLS))
    mix = gates[:, :D_MODEL] * y_conv + gates[:, D_MODEL:] * y_attn
    h = x + jnp.dot(mix.astype(BF16), wout_ref[...], preferred_element_type=F32)
    h_ref[...] = h

    ms2 = jnp.mean(h * h, axis=-1, keepdims=True)
    u2 = h * lax.rsqrt(ms2 + RMS_EPS) * gffn_ref[...]
    for cidx in range(CHUNKS):
        u2_ref[pl.ds(cidx, TQ, stride=CHUNKS), :] = u2[:, cidx * LANES:(cidx + 1) * LANES]
    logits = lax.dot_general(wrt_ref[...], u2, (((1,), (1,)), ((), ())),
                             precision=lax.Precision.HIGHEST,
                             preferred_element_type=F32) + brt_ref[...]
    eidx = lax.broadcasted_iota(jnp.int32, (N_EXPERTS, TQ), 0)
    vals, idxs = [], []
    for _ in range(TOP_K):
        m = jnp.max(logits, axis=0, keepdims=True)
        sel = jnp.min(jnp.where(logits == m, eidx, N_EXPERTS), axis=0, keepdims=True)
        vals.append(m)
        idxs.append(sel)
        logits = jnp.where(eidx == sel, -jnp.inf, logits)
    ex = [jnp.exp(v - vals[0]) for v in vals]
    den = ex[0] + ex[1] + ex[2] + ex[3]
    topi_ref[...] = jnp.concatenate(idxs, axis=0)
    topg_ref[...] = jnp.concatenate([e / den for e in ex], axis=0)


def _const_spec(shape):
    return pl.BlockSpec(shape, lambda b, s: (0,) * len(shape))


def _mixer(x, pos3, gmix, win, wdw, bdw, gln, bln, wco, sinks, wao, wout, gffn, wrt, brt):
    B, S, D = x.shape
    T = B * S
    n_s = S // TQ
    row_map = lambda b, s: (b * n_s + s, 0)
    return pl.pallas_call(
        _mixer_kernel,
        grid=(B, n_s),
        in_specs=[
            pl.BlockSpec((None, TQ, D), lambda b, s: (b, s, 0)),
            pl.BlockSpec((None, TQ, 1), lambda b, s: (b, s, 0)),
            _const_spec((1, D)),
            _const_spec((D, IN_COLS)),
            _const_spec((CONV_WIDTH, D)),
            _const_spec((1, D)),
            _const_spec((1, D)),
            _const_spec((1, D)),
            _const_spec((D, D)),
            pl.BlockSpec(memory_space=pltpu.SMEM),
            _const_spec((D, D)),
            _const_spec((D, D)),
            _const_spec((1, D)),
            _const_spec((N_EXPERTS, D)),
            _const_spec((N_EXPERTS, 1)),
        ],
        out_specs=[
            pl.BlockSpec((TQ, D), row_map),
            pl.BlockSpec((TQ * CHUNKS, LANES), row_map),
            pl.BlockSpec((TOP_K, TQ), lambda b, s: (0, b * n_s + s)),
            pl.BlockSpec((TOP_K, TQ), lambda b, s: (0, b * n_s + s)),
        ],
        out_shape=[
            jax.ShapeDtypeStruct((T, D), F32),
            jax.ShapeDtypeStruct((T * CHUNKS, LANES), F32),
            jax.ShapeDtypeStruct((TOP_K, T), jnp.int32),
            jax.ShapeDtypeStruct((TOP_K, T), F32),
        ],
        scratch_shapes=[
            pltpu.VMEM((TQ + CONV_HALO, D), F32),
            pltpu.VMEM((WINDOW, LANES), BF16),
            pltpu.VMEM((WINDOW, LANES), BF16),
            pltpu.VMEM((TQ, D), F32),
            pltpu.VMEM((TQ, D), BF16),
        ],
        compiler_params=pltpu.CompilerParams(
            dimension_semantics=("arbitrary", "arbitrary"),
            vmem_limit_bytes=VMEM_LIMIT),
        name="mixer",
    )(x, pos3, gmix, win, wdw, bdw, gln, bln, wco, sinks, wao, wout, gffn, wrt, brt)


def _route_kernel(topi_ref, dest_ref, blk_ref, ends_ref, cnt, base, *, n_blk_pad):
    phase = pl.program_id(0)
    step = pl.program_id(1)
    n_sub = ROUTE_TOK // ROUTE_SUB
    eidx = lax.broadcasted_iota(jnp.int32, (N_EXPERTS, ROUTE_SUB), 0)

    def onehot(j, sub):
        idx = topi_ref[j:j + 1, sub * ROUTE_SUB:(sub + 1) * ROUTE_SUB]
        return eidx == idx

    @pl.when((phase == 0) & (step == 0))
    def _():
        cnt[...] = jnp.zeros_like(cnt)

    @pl.when(phase == 0)
    def _():
        tot = jnp.zeros((N_EXPERTS, 1), F32)
        for sub in range(n_sub):
            for j in range(TOP_K):
                tot = tot + jnp.sum(onehot(j, sub).astype(F32), axis=1, keepdims=True)
        cnt[...] = cnt[...] + tot

    @pl.when((phase == 1) & (step == 0))
    def _():
        counts = cnt[...]
        padded = jnp.floor((counts + (BM - 1)) * (1.0 / BM)) * BM
        sub_i = lax.broadcasted_iota(jnp.int32, (N_EXPERTS, LANES), 0)
        lane_i = lax.broadcasted_iota(jnp.int32, (N_EXPERTS, LANES), 1)
        row = jnp.sum(jnp.where(sub_i == lane_i, padded, 0.0), axis=0, keepdims=True)
        ends = jnp.sum(jnp.where(lane_i <= sub_i, row, 0.0), axis=1, keepdims=True)
        base[...] = jnp.broadcast_to(ends, (N_EXPERTS, LANES)) - padded
        ends_row = jnp.sum(jnp.where(sub_i == lane_i, ends, 0.0), axis=0, keepdims=True)
        ends_ref[...] = ends_row.astype(jnp.int32)
        blk_start = (lax.broadcasted_iota(jnp.int32, (N_EXPERTS, n_blk_pad), 1) * BM).astype(F32)
        blk = jnp.sum((blk_start >= ends).astype(jnp.int32), axis=0, keepdims=True)
        blk_ref[...] = jnp.minimum(blk, N_EXPERTS - 1)

    @pl.when(phase == 1)
    def _():
        r = lax.broadcasted_iota(jnp.int32, (ROUTE_SUB, ROUTE_SUB), 0)
        cidx = lax.broadcasted_iota(jnp.int32, (ROUTE_SUB, ROUTE_SUB), 1)
        upper = jnp.where(r < cidx, 1.0, 0.0).astype(BF16)
        run = base[...][:, 0:1]
        for sub in range(n_sub):
            for j in range(TOP_K):
                oh = onehot(j, sub)
                ohf = oh.astype(F32)
                pre = jnp.dot(ohf.astype(BF16), upper, preferred_element_type=F32)
                rank = jnp.sum(jnp.where(oh, pre + run, 0.0), axis=0, keepdims=True)
                dest_ref[j:j + 1, sub * ROUTE_SUB:(sub + 1) * ROUTE_SUB] = rank.astype(jnp.int32)
                run = run + jnp.sum(ohf, axis=1, keepdims=True)
        base[...] = jnp.broadcast_to(run, (N_EXPERTS, LANES))


def _route(topi, n_blk_pad):
    T = topi.shape[1]
    n_steps = T // ROUTE_TOK
    return pl.pallas_call(
        functools.partial(_route_kernel, n_blk_pad=n_blk_pad),
        grid=(2, n_steps),
        in_specs=[pl.BlockSpec((TOP_K, ROUTE_TOK), lambda p, c: (0, c))],
        out_specs=[
            pl.BlockSpec((TOP_K, ROUTE_TOK), lambda p, c: (0, c * p)),
            pl.BlockSpec((1, n_blk_pad), lambda p, c: (0, 0)),
            pl.BlockSpec((1, LANES), lambda p, c: (0, 0)),
        ],
        out_shape=[
            jax.ShapeDtypeStruct((TOP_K, T), jnp.int32),
            jax.ShapeDtypeStruct((1, n_blk_pad), jnp.int32),
            jax.ShapeDtypeStruct((1, LANES), jnp.int32),
        ],
        scratch_shapes=[
            pltpu.VMEM((N_EXPERTS, LANES), F32),
            pltpu.VMEM((N_EXPERTS, LANES), F32),
        ],
        compiler_params=pltpu.CompilerParams(
            dimension_semantics=("arbitrary", "arbitrary")),
        name="route",
    )(topi)


def _slab(ref, row):
    return ref.at[pl.ds(pl.multiple_of(row * CHUNKS, CHUNKS), CHUNKS), :]


def _dispatch_kernel(dest_ref, ends_ref, u2_ref, xs_hbm, zbuf, zsem, sem):
    step = pl.program_id(0)

    @pl.when(step == 0)
    def _():
        zbuf[...] = jnp.zeros_like(zbuf)

        def zero_copy(e):
            start = pl.multiple_of((ends_ref[0, e] - BM) * CHUNKS, BM * CHUNKS)
            return pltpu.make_async_copy(zbuf, xs_hbm.at[pl.ds(start, BM * CHUNKS), :], zsem)

        def nonempty(e):
            prev_end = jnp.where(e > 0, ends_ref[0, jnp.maximum(e - 1, 0)], 0)
            return ends_ref[0, e] > prev_end

        def start_body(e, carry):
            @pl.when(nonempty(e))
            def _():
                zero_copy(e).start()
            return carry

        def wait_body(e, carry):
            @pl.when(nonempty(e))
            def _():
                zero_copy(e).wait()
            return carry

        lax.fori_loop(0, N_EXPERTS, start_body, 0)
        lax.fori_loop(0, N_EXPERTS, wait_body, 0)

        def tail_copy(b):
            start = pl.multiple_of(b * (BM * CHUNKS), BM * CHUNKS)
            return pltpu.make_async_copy(zbuf, xs_hbm.at[pl.ds(start, BM * CHUNKS), :], zsem)

        def tail_start(b, carry):
            tail_copy(b).start()
            return carry

        def tail_wait(b, carry):
            tail_copy(b).wait()
            return carry

        n_used = ends_ref[0, N_EXPERTS - 1] // BM
        n_blk = xs_hbm.shape[0] // (BM * CHUNKS)
        lax.fori_loop(n_used, n_blk, tail_start, 0)
        lax.fori_loop(n_used, n_blk, tail_wait, 0)

    def row_copy(t, j):
        return pltpu.make_async_copy(_slab(u2_ref, t), _slab(xs_hbm, dest_ref[j, t]), sem)

    def start_body(t, carry):
        for j in range(TOP_K):
            row_copy(t, j).start()
        return carry

    def wait_body(t, carry):
        for j in range(TOP_K):
            row_copy(t, j).wait()
        return carry

    lax.fori_loop(0, DISPATCH_TOK, start_body, 0)
    lax.fori_loop(0, DISPATCH_TOK, wait_body, 0)


def _dispatch(dest, ends, u2s, n_rows):
    T = dest.shape[1]
    return pl.pallas_call(
        _dispatch_kernel,
        grid=(T // DISPATCH_TOK,),
        in_specs=[
            pl.BlockSpec((TOP_K, DISPATCH_TOK), lambda i: (0, i), memory_space=pltpu.SMEM),
            pl.BlockSpec(memory_space=pltpu.SMEM),
            pl.BlockSpec((DISPATCH_TOK * CHUNKS, LANES), lambda i: (i, 0)),
        ],
        out_specs=pl.BlockSpec(memory_space=pl.ANY),
        out_shape=jax.ShapeDtypeStruct((n_rows * CHUNKS, LANES), F32),
        scratch_shapes=[
            pltpu.VMEM((BM * CHUNKS, LANES), F32),
            pltpu.SemaphoreType.DMA,
            pltpu.SemaphoreType.DMA,
        ],
        compiler_params=pltpu.CompilerParams(dimension_semantics=("arbitrary",)),
        name="dispatch",
    )(dest, ends, u2s)


def _expert_kernel(blk_ref, ends_ref, xs_ref, w1_ref, b1_ref, w2_ref, b2_ref, ys_ref):
    i = pl.program_id(0)
    n_used = ends_ref[0, N_EXPERTS - 1] // BM

    @pl.when(i < n_used)
    def _():
        xb = jnp.concatenate(
            [xs_ref[pl.ds(cidx, BM, stride=CHUNKS), :] for cidx in range(CHUNKS)],
            axis=1).astype(BF16)
        h1 = jnp.dot(xb, w1_ref[...], preferred_element_type=F32) + b1_ref[...]
        f = h1.shape[1] // 2
        glu = jnp.minimum(h1[:, :f], SWIGLU_LIMIT)
        lin = jnp.clip(h1[:, f:], -SWIGLU_LIMIT, SWIGLU_LIMIT)
        act = glu * _sigmoid(SWIGLU_ALPHA * glu) * (lin + 1.0)
        y = jnp.dot(act.astype(BF16), w2_ref[...], preferred_element_type=F32) + b2_ref[...]
        for cidx in range(CHUNKS):
            ys_ref[pl.ds(cidx, BM, stride=CHUNKS), :] = y[:, cidx * LANES:(cidx + 1) * LANES]

    @pl.when(i >= n_used)
    def _():
        ys_ref[...] = jnp.zeros_like(ys_ref)


def _experts(blk_e, ends, xs, w1, b1, w2, b2, n_blk):
    E, D, F2 = w1.shape

    def used(i, ends_ref):
        return jnp.minimum(i, ends_ref[0, N_EXPERTS - 1] // BM - 1)

    grid_spec = pltpu.PrefetchScalarGridSpec(
        num_scalar_prefetch=2,
        grid=(n_blk,),
        in_specs=[
            pl.BlockSpec((BM * CHUNKS, LANES), lambda i, blk, ends: (used(i, ends), 0)),
            pl.BlockSpec((None, D, F2), lambda i, blk, ends: (blk[0, used(i, ends)], 0, 0)),
            pl.BlockSpec((None, 1, F2), lambda i, blk, ends: (blk[0, used(i, ends)], 0, 0)),
            pl.BlockSpec((None, F2 // 2, D), lambda i, blk, ends: (blk[0, used(i, ends)], 0, 0)),
            pl.BlockSpec((None, 1, D), lambda i, blk, ends: (blk[0, used(i, ends)], 0, 0)),
        ],
        out_specs=pl.BlockSpec((BM * CHUNKS, LANES), lambda i, blk, ends: (i, 0)),
    )
    return pl.pallas_call(
        _expert_kernel,
        grid_spec=grid_spec,
        out_shape=jax.ShapeDtypeStruct((n_blk * BM * CHUNKS, LANES), F32),
        compiler_params=pltpu.CompilerParams(
            dimension_semantics=("arbitrary",), vmem_limit_bytes=VMEM_LIMIT),
        name="experts",
    )(blk_e, ends, xs, w1, b1, w2, b2)


def _combine_kernel(dest_ref, gate_ref, h_ref, gfin_ref, ys_hbm, out_ref, gbuf, sem):
    def row_copy(t, j):
        return pltpu.make_async_copy(
            _slab(ys_hbm, dest_ref[j, t]), _slab(gbuf, j * COMBINE_TOK + t), sem)

    def start_body(t, carry):
        for j in range(TOP_K):
            row_copy(t, j).start()
        return carry

    def wait_body(t, carry):
        for j in range(TOP_K):
            row_copy(t, j).wait()
        return carry

    lax.fori_loop(0, COMBINE_TOK, start_body, 0)
    lax.fori_loop(0, COMBINE_TOK, wait_body, 0)

    gate = gate_ref[...]
    parts = []
    ss = jnp.zeros((COMBINE_TOK, 1), F32)
    for cidx in range(CHUNKS):
        acc = h_ref[:, cidx * LANES:(cidx + 1) * LANES]
        for j in range(TOP_K):
            rows = gbuf[pl.ds(j * COMBINE_TOK * CHUNKS + cidx, COMBINE_TOK, stride=CHUNKS), :]
            acc = acc + gate[:, j:j + 1] * rows
        parts.append(acc)
        ss = ss + jnp.sum(acc * acc, axis=-1, keepdims=True)
    scale = lax.rsqrt(ss * (1.0 / D_MODEL) + RMS_EPS)
    for cidx in range(CHUNKS):
        lanes = slice(cidx * LANES, (cidx + 1) * LANES)
        out_ref[:, lanes] = parts[cidx] * scale * gfin_ref[:, lanes]


def _combine(dest, gate_t, h, gfin, ys):
    T, D = h.shape
    return pl.pallas_call(
        _combine_kernel,
        grid=(T // COMBINE_TOK,),
        in_specs=[
            pl.BlockSpec((TOP_K, COMBINE_TOK), lambda i: (0, i), memory_space=pltpu.SMEM),
            pl.BlockSpec((COMBINE_TOK, TOP_K), lambda i: (i, 0)),
            pl.BlockSpec((COMBINE_TOK, D), lambda i: (i, 0)),
            pl.BlockSpec((1, D), lambda i: (0, 0)),
            pl.BlockSpec(memory_space=pl.ANY),
        ],
        out_specs=pl.BlockSpec((COMBINE_TOK, D), lambda i: (i, 0)),
        out_shape=jax.ShapeDtypeStruct((T, D), F32),
        scratch_shapes=[
            pltpu.VMEM((TOP_K * COMBINE_TOK * CHUNKS, LANES), F32),
            pltpu.SemaphoreType.DMA,
        ],
        compiler_params=pltpu.CompilerParams(dimension_semantics=("arbitrary",)),
        name="combine",
    )(dest, gate_t, h, gfin, ys)


def kernel(x, positions, g_mix_norm, w_in, w_dw, b_dw, g_conv_ln, b_conv_ln, w_conv_out,
           attn_sinks, w_attn_out, w_out, g_ffn_norm, w_router, b_router,
           w_mlp1, b_mlp1, w_mlp2, b_mlp2, g_final):
    B, S, D = x.shape
    T = B * S
    depth = g_mix_norm.shape[0]
    assert depth == 1 and D == D_MODEL and S % TQ == 0 and T % ROUTE_TOK == 0
    l = 0
    n_blk = (T * TOP_K) // BM + N_EXPERTS
    n_blk_pad = -(-n_blk // LANES) * LANES

    h, u2s, topi, topg = _mixer(
        x, positions[..., None], g_mix_norm[l][None, :],
        w_in[l].astype(BF16), w_dw[l], b_dw[l][None, :], g_conv_ln[l][None, :],
        b_conv_ln[l][None, :], w_conv_out[l].astype(BF16), attn_sinks[l],
        w_attn_out[l].astype(BF16), w_out[l].astype(BF16), g_ffn_norm[l][None, :],
        w_router[l].T, b_router[l][:, None])
    dest, blk_e, ends = _route(topi, n_blk_pad)
    xs = _dispatch(dest, ends, u2s, n_blk * BM)
    ys = _experts(blk_e, ends, xs, w_mlp1[l].astype(BF16), b_mlp1[l][:, None, :],
                  w_mlp2[l].astype(BF16), b_mlp2[l][:, None, :], n_blk)
    out = _combine(dest, topg.T, h, g_final[None, :], ys)
    return out.reshape(B, S, D)
```

```python
import functools

import jax
import jax.numpy as jnp
import numpy as np
from jax import lax
from jax.experimental import pallas as pl
from jax.experimental.pallas import tpu as pltpu

F32 = jnp.float32
BF16 = jnp.bfloat16

D_MODEL = 1024
CONV_WIDTH = 31
HEAD_DIM = 64
N_Q_HEADS = 16
N_KV_HEADS = 2
WINDOW = 128
ROT_DIM = 16
ROPE_THETA = 500000.0
N_EXPERTS = 32
TOP_K = 4
SWIGLU_LIMIT = 7.0
SWIGLU_ALPHA = 1.702
RMS_EPS = 1e-5
LN_EPS = 1e-5

LANES = 128
SUBLANES = 8
CHUNKS = D_MODEL // LANES

COL_VAL = 0
COL_GATE = D_MODEL
COL_Q = 2 * D_MODEL
COL_K = COL_Q + N_Q_HEADS * HEAD_DIM
COL_V = COL_K + N_KV_HEADS * HEAD_DIM
COL_G = COL_V + N_KV_HEADS * HEAD_DIM
IN_COLS = COL_G + 2 * D_MODEL

TQ = 256
CONV_HALO = 32
ROUTE_TOK = 2048
ROUTE_SUB = 512
BM = 256
DISPATCH_TOK = 512
COMBINE_TOK = 256
VMEM_LIMIT = 56 * 1024 * 1024


def _sigmoid(v):
    return 1.0 / (1.0 + jnp.exp(-v))


def _mixer_kernel(x_ref, pos_ref, gmix_ref, win_ref, wqkvt_ref, wdw_ref, bdw_ref, gln_ref,
                  bln_ref, wco_ref, sinks_ref, wao_ref, wout_ref, gffn_ref, wrt_ref, brt_ref,
                  h_ref, u2_ref, topi_ref, topg_ref,
                  cbuf, kprev, vprev, conv_scr, o_scr):
    s = pl.program_id(1)

    @pl.when(s == 0)
    def _():
        cbuf[0:CONV_HALO, :] = jnp.zeros((CONV_HALO, D_MODEL), F32)
        kprev[...] = jnp.zeros_like(kprev)
        vprev[...] = jnp.zeros_like(vprev)

    x = x_ref[...]
    ms = jnp.mean(x * x, axis=-1, keepdims=True)
    u = (x * lax.rsqrt(ms + RMS_EPS) * gmix_ref[...]).astype(BF16)

    def proj(lo, hi):
        return jnp.dot(u, win_ref[:, lo:hi], preferred_element_type=F32)

    c = proj(COL_VAL, COL_GATE) * _sigmoid(proj(COL_GATE, COL_Q))
    cbuf[CONV_HALO:CONV_HALO + TQ, :] = c
    n_buf = TQ + CONV_HALO
    for lt in range(CHUNKS):
        lanes = slice(lt * LANES, (lt + 1) * LANES)
        col = cbuf[:, lanes]
        shifted = [col] + [pltpu.roll(col, n_buf - r, 0) for r in range(1, SUBLANES)]
        acc = jnp.zeros((TQ, LANES), F32)
        for j in range(CONV_WIDTH):
            blk, r = divmod(CONV_HALO - (CONV_WIDTH - 1) + j, SUBLANES)
            acc = acc + wdw_ref[j:j + 1, lanes] * shifted[r][blk * SUBLANES:blk * SUBLANES + TQ]
        conv_scr[:, lanes] = acc
    cbuf[0:CONV_HALO, :] = cbuf[TQ:TQ + CONV_HALO, :]
    cv = conv_scr[...] + bdw_ref[...]
    mu = jnp.mean(cv, axis=-1, keepdims=True)
    cc = cv - mu
    var = jnp.mean(cc * cc, axis=-1, keepdims=True)
    cn = cc * lax.rsqrt(var + LN_EPS) * gln_ref[...] + bln_ref[...]
    cn = cn * _sigmoid(cn)
    y_conv = jnp.dot(cn.astype(BF16), wco_ref[...], preferred_element_type=F32)

    nt = (((1,), (1,)), ((), ()))
    tn = (((0,), (0,)), ((), ()))
    qkv_t = lax.dot_general(wqkvt_ref[...], u, nt, preferred_element_type=F32)
    half = ROT_DIM // 2
    pair = lax.broadcasted_iota(jnp.int32, (half, TQ), 0).astype(F32) * 2.0
    inv_freq = jnp.power(jnp.float32(ROPE_THETA), -pair / ROT_DIM)
    ang = pos_ref[...].astype(F32) * inv_freq
    cos = jnp.cos(ang)
    sin = jnp.sin(ang)

    def rotary(t):
        x1, x2 = t[0:half], t[half:ROT_DIM]
        return jnp.concatenate([x1 * cos - x2 * sin, x2 * cos + x1 * sin, t[ROT_DIM:]], axis=0)

    def head_rows(h):
        return slice(h * HEAD_DIM, (h + 1) * HEAD_DIM)

    k_row0 = N_Q_HEADS * HEAD_DIM
    v_row0 = k_row0 + N_KV_HEADS * HEAD_DIM
    q_t = jnp.concatenate([rotary(qkv_t[head_rows(h)]) for h in range(N_Q_HEADS)],
                          axis=0).astype(BF16)
    scale = HEAD_DIM ** -0.5
    k_t = jnp.concatenate(
        [rotary(qkv_t[k_row0 + g * HEAD_DIM:k_row0 + (g + 1) * HEAD_DIM]) * scale
         for g in range(N_KV_HEADS)], axis=0).astype(BF16)
    v_t = qkv_t[v_row0:v_row0 + N_KV_HEADS * HEAD_DIM].astype(BF16)

    group = N_Q_HEADS // N_KV_HEADS
    ki = lax.broadcasted_iota(jnp.int32, (2 * WINDOW, group * WINDOW), 0)
    qi = lax.broadcasted_iota(jnp.int32, (2 * WINDOW, group * WINDOW), 1) % WINDOW
    band = (ki > qi) & (ki <= qi + WINDOW)
    for qb in range(TQ // WINDOW):
        cols = slice(qb * WINDOW, (qb + 1) * WINDOW)
        if qb == 0:
            k_prev, v_prev = kprev[...], vprev[...]
            mask = band & (ki >= jnp.where(s > 0, 0, WINDOW))
        else:
            prev = slice((qb - 1) * WINDOW, qb * WINDOW)
            k_prev, v_prev = k_t[:, prev], v_t[:, prev]
            mask = band
        for g in range(N_KV_HEADS):
            k_cat = jnp.concatenate([k_prev[head_rows(g)], k_t[head_rows(g), cols]], axis=1)
            v_cat = jnp.concatenate([v_prev[head_rows(g)], v_t[head_rows(g), cols]], axis=1)
            q_blk = jnp.concatenate(
                [q_t[head_rows(g * group + i), cols] for i in range(group)], axis=1)
            sc = lax.dot_general(k_cat, q_blk, tn, preferred_element_type=F32)
            sc = jnp.where(mask, sc, -jnp.inf)
            sink = jnp.concatenate(
                [jnp.full((1, WINDOW), sinks_ref[g * group + i], F32) for i in range(group)],
                axis=1)
            m = jnp.maximum(jnp.max(sc, axis=0, keepdims=True), sink)
            e = jnp.exp(sc - m)
            den = jnp.sum(e, axis=0, keepdims=True) + jnp.exp(sink - m)
            p_t = (e * (1.0 / den)).astype(BF16)
            o_t = jnp.dot(v_cat, p_t, preferred_element_type=F32)
            for i in range(group):
                o_scr[head_rows(g * group + i), cols] = (
                    o_t[:, i * WINDOW:(i + 1) * WINDOW].astype(BF16))
    last = slice(TQ - WINDOW, TQ)
    kprev[...] = k_t[:, last]
    vprev[...] = v_t[:, last]
    y_attn = lax.dot_general(o_scr[...], wao_ref[...], tn, preferred_element_type=F32)

    gates = _sigmoid(proj(COL_G, IN_COLS))
    mix = gates[:, :D_MODEL] * y_conv + gates[:, D_MODEL:] * y_attn
    h = x + jnp.dot(mix.astype(BF16), wout_ref[...], preferred_element_type=F32)
    h_ref[...] = h

    ms2 = jnp.mean(h * h, axis=-1, keepdims=True)
    u2 = h * lax.rsqrt(ms2 + RMS_EPS) * gffn_ref[...]
    for cidx in range(CHUNKS):
        u2_ref[pl.ds(cidx, TQ, stride=CHUNKS), :] = u2[:, cidx * LANES:(cidx + 1) * LANES]
    logits = lax.dot_general(wrt_ref[...], u2, (((1,), (1,)), ((), ())),
                             precision=lax.Precision.HIGHEST,
                             preferred_element_type=F32) + brt_ref[...]
    eidx = lax.broadcasted_iota(jnp.int32, (N_EXPERTS, TQ), 0)
    vals, idxs = [], []
    for _ in range(TOP_K):
        m = jnp.max(logits, axis=0, keepdims=True)
        sel = jnp.min(jnp.where(logits == m, eidx, N_EXPERTS), axis=0, keepdims=True)
        vals.append(m)
        idxs.append(sel)
        logits = jnp.where(eidx == sel, -jnp.inf, logits)
    ex = [jnp.exp(v - vals[0]) for v in vals]
    den = ex[0] + ex[1] + ex[2] + ex[3]
    topi_ref[...] = jnp.concatenate(idxs, axis=0)
    topg_ref[...] = jnp.concatenate([e / den for e in ex], axis=0)


def _const_spec(shape):
    return pl.BlockSpec(shape, lambda b, s: (0,) * len(shape))


def _mixer(x, pos3, gmix, win, wqkvt, wdw, bdw, gln, bln, wco, sinks, wao, wout, gffn, wrt, brt):
    B, S, D = x.shape
    T = B * S
    n_s = S // TQ
    row_map = lambda b, s: (b * n_s + s, 0)
    return pl.pallas_call(
        _mixer_kernel,
        grid=(B, n_s),
        in_specs=[
            pl.BlockSpec((None, TQ, D), lambda b, s: (b, s, 0)),
            pl.BlockSpec((None, 1, TQ), lambda b, s: (b, 0, s)),
            _const_spec((1, D)),
            _const_spec((D, IN_COLS)),
            _const_spec((COL_G - COL_Q, D)),
            _const_spec((CONV_WIDTH, D)),
            _const_spec((1, D)),
            _const_spec((1, D)),
            _const_spec((1, D)),
            _const_spec((D, D)),
            pl.BlockSpec(memory_space=pltpu.SMEM),
            _const_spec((D, D)),
            _const_spec((D, D)),
            _const_spec((1, D)),
            _const_spec((N_EXPERTS, D)),
            _const_spec((N_EXPERTS, 1)),
        ],
        out_specs=[
            pl.BlockSpec((TQ, D), row_map),
            pl.BlockSpec((TQ * CHUNKS, LANES), row_map),
            pl.BlockSpec((TOP_K, TQ), lambda b, s: (0, b * n_s + s)),
            pl.BlockSpec((TOP_K, TQ), lambda b, s: (0, b * n_s + s)),
        ],
        out_shape=[
            jax.ShapeDtypeStruct((T, D), F32),
            jax.ShapeDtypeStruct((T * CHUNKS, LANES), F32),
            jax.ShapeDtypeStruct((TOP_K, T), jnp.int32),
            jax.ShapeDtypeStruct((TOP_K, T), F32),
        ],
        scratch_shapes=[
            pltpu.VMEM((TQ + CONV_HALO, D), F32),
            pltpu.VMEM((WINDOW, LANES), BF16),
            pltpu.VMEM((WINDOW, LANES), BF16),
            pltpu.VMEM((TQ, D), F32),
            pltpu.VMEM((D, TQ), BF16),
        ],
        compiler_params=pltpu.CompilerParams(
            dimension_semantics=("arbitrary", "arbitrary"),
            vmem_limit_bytes=VMEM_LIMIT),
        name="mixer",
    )(x, pos3, gmix, win, wqkvt, wdw, bdw, gln, bln, wco, sinks, wao, wout, gffn, wrt, brt)


def _route_kernel(topi_ref, dest_ref, blk_ref, ends_ref, cnt, base, *, n_blk_pad):
    phase = pl.program_id(0)
    step = pl.program_id(1)
    n_sub = ROUTE_TOK // ROUTE_SUB
    eidx = lax.broadcasted_iota(jnp.int32, (N_EXPERTS, ROUTE_SUB), 0)

    def onehot(j, sub):
        idx = topi_ref[j:j + 1, sub * ROUTE_SUB:(sub + 1) * ROUTE_SUB]
        return eidx == idx

    @pl.when((phase == 0) & (step == 0))
    def _():
        cnt[...] = jnp.zeros_like(cnt)

    @pl.when(phase == 0)
    def _():
        tot = jnp.zeros((N_EXPERTS, 1), F32)
        for sub in range(n_sub):
            for j in range(TOP_K):
                tot = tot + jnp.sum(onehot(j, sub).astype(F32), axis=1, keepdims=True)
        cnt[...] = cnt[...] + tot

    @pl.when((phase == 1) & (step == 0))
    def _():
        counts = cnt[...]
        padded = jnp.floor((counts + (BM - 1)) * (1.0 / BM)) * BM
        sub_i = lax.broadcasted_iota(jnp.int32, (N_EXPERTS, LANES), 0)
        lane_i = lax.broadcasted_iota(jnp.int32, (N_EXPERTS, LANES), 1)
        row = jnp.sum(jnp.where(sub_i == lane_i, padded, 0.0), axis=0, keepdims=True)
        ends = jnp.sum(jnp.where(lane_i <= sub_i, row, 0.0), axis=1, keepdims=True)
        base[...] = jnp.broadcast_to(ends, (N_EXPERTS, LANES)) - padded
        ends_row = jnp.sum(jnp.where(sub_i == lane_i, ends, 0.0), axis=0, keepdims=True)
        ends_ref[...] = ends_row.astype(jnp.int32)
        blk_start = (lax.broadcasted_iota(jnp.int32, (N_EXPERTS, n_blk_pad), 1) * BM).astype(F32)
        blk = jnp.sum((blk_start >= ends).astype(jnp.int32), axis=0, keepdims=True)
        blk_ref[...] = jnp.minimum(blk, N_EXPERTS - 1)

    @pl.when(phase == 1)
    def _():
        r = lax.broadcasted_iota(jnp.int32, (ROUTE_SUB, ROUTE_SUB), 0)
        cidx = lax.broadcasted_iota(jnp.int32, (ROUTE_SUB, ROUTE_SUB), 1)
        upper = jnp.where(r < cidx, 1.0, 0.0).astype(BF16)
        run = base[...][:, 0:1]
        for sub in range(n_sub):
            for j in range(TOP_K):
                oh = onehot(j, sub)
                ohf = oh.astype(F32)
                pre = jnp.dot(ohf.astype(BF16), upper, preferred_element_type=F32)
                rank = jnp.sum(jnp.where(oh, pre + run, 0.0), axis=0, keepdims=True)
                dest_ref[j:j + 1, sub * ROUTE_SUB:(sub + 1) * ROUTE_SUB] = rank.astype(jnp.int32)
                run = run + jnp.sum(ohf, axis=1, keepdims=True)
        base[...] = jnp.broadcast_to(run, (N_EXPERTS, LANES))


def _route(topi, n_blk_pad):
    T = topi.shape[1]
    n_steps = T // ROUTE_TOK
    return pl.pallas_call(
        functools.partial(_route_kernel, n_blk_pad=n_blk_pad),
        grid=(2, n_steps),
        in_specs=[pl.BlockSpec((TOP_K, ROUTE_TOK), lambda p, c: (0, c))],
        out_specs=[
            pl.BlockSpec((TOP_K, ROUTE_TOK), lambda p, c: (0, c * p)),
            pl.BlockSpec((1, n_blk_pad), lambda p, c: (0, 0)),
            pl.BlockSpec((1, LANES), lambda p, c: (0, 0)),
        ],
        out_shape=[
            jax.ShapeDtypeStruct((TOP_K, T), jnp.int32),
            jax.ShapeDtypeStruct((1, n_blk_pad), jnp.int32),
            jax.ShapeDtypeStruct((1, LANES), jnp.int32),
        ],
        scratch_shapes=[
            pltpu.VMEM((N_EXPERTS, LANES), F32),
            pltpu.VMEM((N_EXPERTS, LANES), F32),
        ],
        compiler_params=pltpu.CompilerParams(
            dimension_semantics=("arbitrary", "arbitrary")),
        name="route",
    )(topi)


def _slab(ref, row):
    return ref.at[pl.ds(pl.multiple_of(row * CHUNKS, CHUNKS), CHUNKS), :]


def _dispatch_kernel(dest_ref, ends_ref, u2_ref, xs_hbm, zbuf, zsem, sem):
    step = pl.program_id(0)

    @pl.when(step == 0)
    def _():
        zbuf[...] = jnp.zeros_like(zbuf)

        def zero_copy(e):
            start = pl.multiple_of((ends_ref[0, e] - BM) * CHUNKS, BM * CHUNKS)
            return pltpu.make_async_copy(zbuf, xs_hbm.at[pl.ds(start, BM * CHUNKS), :], zsem)

        def nonempty(e):
            prev_end = jnp.where(e > 0, ends_ref[0, jnp.maximum(e - 1, 0)], 0)
            return ends_ref[0, e] > prev_end

        def start_body(e, carry):
            @pl.when(nonempty(e))
            def _():
                zero_copy(e).start()
            return carry

        def wait_body(e, carry):
            @pl.when(nonempty(e))
            def _():
                zero_copy(e).wait()
            return carry

        lax.fori_loop(0, N_EXPERTS, start_body, 0)
        lax.fori_loop(0, N_EXPERTS, wait_body, 0)

        def tail_copy(b):
            start = pl.multiple_of(b * (BM * CHUNKS), BM * CHUNKS)
            return pltpu.make_async_copy(zbuf, xs_hbm.at[pl.ds(start, BM * CHUNKS), :], zsem)

        def tail_start(b, carry):
            tail_copy(b).start()
            return carry

        def tail_wait(b, carry):
            tail_copy(b).wait()
            return carry

        n_used = ends_ref[0, N_EXPERTS - 1] // BM
        n_blk = xs_hbm.shape[0] // (BM * CHUNKS)
        lax.fori_loop(n_used, n_blk, tail_start, 0)
        lax.fori_loop(n_used, n_blk, tail_wait, 0)

    def row_copy(t, j):
        return pltpu.make_async_copy(_slab(u2_ref, t), _slab(xs_hbm, dest_ref[j, t]), sem)

    def start_body(t, carry):
        for j in range(TOP_K):
            row_copy(t, j).start()
        return carry

    def wait_body(t, carry):
        for j in range(TOP_K):
            row_copy(t, j).wait()
        return carry

    lax.fori_loop(0, DISPATCH_TOK, start_body, 0)
    lax.fori_loop(0, DISPATCH_TOK, wait_body, 0)


def _dispatch(dest, ends, u2s, n_rows):
    T = dest.shape[1]
    return pl.pallas_call(
        _dispatch_kernel,
        grid=(T // DISPATCH_TOK,),
        in_specs=[
            pl.BlockSpec((TOP_K, DISPATCH_TOK), lambda i: (0, i), memory_space=pltpu.SMEM),
            pl.BlockSpec(memory_space=pltpu.SMEM),
            pl.BlockSpec((DISPATCH_TOK * CHUNKS, LANES), lambda i: (i, 0)),
        ],
        out_specs=pl.BlockSpec(memory_space=pl.ANY),
        out_shape=jax.ShapeDtypeStruct((n_rows * CHUNKS, LANES), F32),
        scratch_shapes=[
            pltpu.VMEM((BM * CHUNKS, LANES), F32),
            pltpu.SemaphoreType.DMA,
            pltpu.SemaphoreType.DMA,
        ],
        compiler_params=pltpu.CompilerParams(dimension_semantics=("arbitrary",)),
        name="dispatch",
    )(dest, ends, u2s)


def _expert_kernel(blk_ref, ends_ref, xs_ref, w1_ref, b1_ref, w2_ref, b2_ref, ys_ref,
                   w1_bf, w2_bf):
    i = pl.program_id(0)
    n_used = ends_ref[0, N_EXPERTS - 1] // BM

    new_expert = (i == 0) | (blk_ref[0, i] != blk_ref[0, jnp.maximum(i - 1, 0)])

    @pl.when((i < n_used) & new_expert)
    def _():
        w1_bf[...] = w1_ref[...].astype(BF16)
        w2_bf[...] = w2_ref[...].astype(BF16)

    @pl.when(i < n_used)
    def _():
        xb = jnp.concatenate(
            [xs_ref[pl.ds(cidx, BM, stride=CHUNKS), :] for cidx in range(CHUNKS)],
            axis=1).astype(BF16)
        h1 = jnp.dot(xb, w1_bf[...], preferred_element_type=F32) + b1_ref[...]
        f = h1.shape[1] // 2
        glu = jnp.minimum(h1[:, :f], SWIGLU_LIMIT)
        lin = jnp.clip(h1[:, f:], -SWIGLU_LIMIT, SWIGLU_LIMIT)
        act = glu * _sigmoid(SWIGLU_ALPHA * glu) * (lin + 1.0)
        y = jnp.dot(act.astype(BF16), w2_bf[...], preferred_element_type=F32) + b2_ref[...]
        for cidx in range(CHUNKS):
            ys_ref[pl.ds(cidx, BM, stride=CHUNKS), :] = y[:, cidx * LANES:(cidx + 1) * LANES]

    @pl.when(i >= n_used)
    def _():
        ys_ref[...] = jnp.zeros_like(ys_ref)


def _experts(blk_e, ends, xs, w1, b1, w2, b2, n_blk):
    E, D, F2 = w1.shape

    def used(i, ends_ref):
        return jnp.minimum(i, ends_ref[0, N_EXPERTS - 1] // BM - 1)

    grid_spec = pltpu.PrefetchScalarGridSpec(
        num_scalar_prefetch=2,
        grid=(n_blk,),
        in_specs=[
            pl.BlockSpec((BM * CHUNKS, LANES), lambda i, blk, ends: (used(i, ends), 0)),
            pl.BlockSpec((None, D, F2), lambda i, blk, ends: (blk[0, used(i, ends)], 0, 0)),
            pl.BlockSpec((None, 1, F2), lambda i, blk, ends: (blk[0, used(i, ends)], 0, 0)),
            pl.BlockSpec((None, F2 // 2, D), lambda i, blk, ends: (blk[0, used(i, ends)], 0, 0)),
            pl.BlockSpec((None, 1, D), lambda i, blk, ends: (blk[0, used(i, ends)], 0, 0)),
        ],
        out_specs=pl.BlockSpec((BM * CHUNKS, LANES), lambda i, blk, ends: (i, 0)),
        scratch_shapes=[
            pltpu.VMEM((D, F2), BF16),
            pltpu.VMEM((F2 // 2, D), BF16),
        ],
    )
    return pl.pallas_call(
        _expert_kernel,
        grid_spec=grid_spec,
        out_shape=jax.ShapeDtypeStruct((n_blk * BM * CHUNKS, LANES), F32),
        compiler_params=pltpu.CompilerParams(
            dimension_semantics=("arbitrary",), vmem_limit_bytes=VMEM_LIMIT),
        name="experts",
    )(blk_e, ends, xs, w1, b1, w2, b2)


def _combine_kernel(dest_ref, gate_ref, h_ref, gfin_ref, ys_hbm, out_ref, gbuf, sem):
    def row_copy(t, j):
        return pltpu.make_async_copy(
            _slab(ys_hbm, dest_ref[j, t]), _slab(gbuf, j * COMBINE_TOK + t), sem)

    def start_body(t, carry):
        for j in range(TOP_K):
            row_copy(t, j).start()
        return carry

    def wait_body(t, carry):
        for j in range(TOP_K):
            row_copy(t, j).wait()
        return carry

    lax.fori_loop(0, COMBINE_TOK, start_body, 0)
    lax.fori_loop(0, COMBINE_TOK, wait_body, 0)

    gate = gate_ref[...]
    parts = []
    ss = jnp.zeros((COMBINE_TOK, 1), F32)
    for cidx in range(CHUNKS):
        acc = h_ref[:, cidx * LANES:(cidx + 1) * LANES]
        for j in range(TOP_K):
            rows = gbuf[pl.ds(j * COMBINE_TOK * CHUNKS + cidx, COMBINE_TOK, stride=CHUNKS), :]
            acc = acc + gate[:, j:j + 1] * rows
        parts.append(acc)
        ss = ss + jnp.sum(acc * acc, axis=-1, keepdims=True)
    scale = lax.rsqrt(ss * (1.0 / D_MODEL) + RMS_EPS)
    for cidx in range(CHUNKS):
        lanes = slice(cidx * LANES, (cidx + 1) * LANES)
        out_ref[:, lanes] = parts[cidx] * scale * gfin_ref[:, lanes]


def _combine(dest, gate_t, h, gfin, ys):
    T, D = h.shape
    return pl.pallas_call(
        _combine_kernel,
        grid=(T // COMBINE_TOK,),
        in_specs=[
            pl.BlockSpec((TOP_K, COMBINE_TOK), lambda i: (0, i), memory_space=pltpu.SMEM),
            pl.BlockSpec((COMBINE_TOK, TOP_K), lambda i: (i, 0)),
            pl.BlockSpec((COMBINE_TOK, D), lambda i: (i, 0)),
            pl.BlockSpec((1, D), lambda i: (0, 0)),
            pl.BlockSpec(memory_space=pl.ANY),
        ],
        out_specs=pl.BlockSpec((COMBINE_TOK, D), lambda i: (i, 0)),
        out_shape=jax.ShapeDtypeStruct((T, D), F32),
        scratch_shapes=[
            pltpu.VMEM((TOP_K * COMBINE_TOK * CHUNKS, LANES), F32),
            pltpu.SemaphoreType.DMA,
        ],
        compiler_params=pltpu.CompilerParams(dimension_semantics=("arbitrary",)),
        name="combine",
    )(dest, gate_t, h, gfin, ys)


def kernel(x, positions, g_mix_norm, w_in, w_dw, b_dw, g_conv_ln, b_conv_ln, w_conv_out,
           attn_sinks, w_attn_out, w_out, g_ffn_norm, w_router, b_router,
           w_mlp1, b_mlp1, w_mlp2, b_mlp2, g_final):
    B, S, D = x.shape
    T = B * S
    depth = g_mix_norm.shape[0]
    assert depth == 1 and D == D_MODEL and S % TQ == 0 and T % ROUTE_TOK == 0
    l = 0
    n_blk = (T * TOP_K) // BM + N_EXPERTS
    n_blk_pad = -(-n_blk // LANES) * LANES

    h, u2s, topi, topg = _mixer(
        x, positions[:, None, :], g_mix_norm[l][None, :],
        w_in[l].astype(BF16), w_in[l][:, COL_Q:COL_G].T.astype(BF16), w_dw[l],
        b_dw[l][None, :], g_conv_ln[l][None, :], b_conv_ln[l][None, :],
        w_conv_out[l].astype(BF16), attn_sinks[l],
        w_attn_out[l].astype(BF16), w_out[l].astype(BF16), g_ffn_norm[l][None, :],
        w_router[l].T, b_router[l][:, None])
    dest, blk_e, ends = _route(topi, n_blk_pad)
    xs = _dispatch(dest, ends, u2s, n_blk * BM)
    ys = _experts(blk_e, ends, xs, w_mlp1[l], b_mlp1[l][:, None, :],
                  w_mlp2[l], b_mlp2[l][:, None, :], n_blk)
    out = _combine(dest, topg.T, h, g_final[None, :], ys)
    return out.reshape(B, S, D)
```

```python
import functools

import jax
import jax.numpy as jnp
import numpy as np
from jax import lax
from jax.experimental import pallas as pl
from jax.experimental.pallas import tpu as pltpu

F32 = jnp.float32
BF16 = jnp.bfloat16

D_MODEL = 1024
CONV_WIDTH = 31
HEAD_DIM = 64
N_Q_HEADS = 16
N_KV_HEADS = 2
WINDOW = 128
ROT_DIM = 16
ROPE_THETA = 500000.0
N_EXPERTS = 32
TOP_K = 4
SWIGLU_LIMIT = 7.0
SWIGLU_ALPHA = 1.702
RMS_EPS = 1e-5
LN_EPS = 1e-5

LANES = 128
SUBLANES = 8
CHUNKS = D_MODEL // LANES

COL_VAL = 0
COL_GATE = D_MODEL
COL_Q = 2 * D_MODEL
COL_K = COL_Q + N_Q_HEADS * HEAD_DIM
COL_V = COL_K + N_KV_HEADS * HEAD_DIM
COL_G = COL_V + N_KV_HEADS * HEAD_DIM
IN_COLS = COL_G + 2 * D_MODEL

TQ = 256
CONV_HALO = 32
ROUTE_TOK = 2048
ROUTE_SUB = 512
BM = 256
EXPERT_SUB = 128
DISPATCH_TOK = 512
COMBINE_TOK = 256
VMEM_LIMIT = 56 * 1024 * 1024


def _sigmoid(v):
    return 1.0 / (1.0 + jnp.exp(-v))


def _mixer_kernel(x_ref, pos_ref, gmix_ref, win_ref, wqkvt_ref, wdw_ref, bdw_ref, gln_ref,
                  bln_ref, wco_ref, sinks_ref, wao_ref, wout_ref, gffn_ref, wrt_ref, brt_ref,
                  h_ref, u2_ref, topi_ref, topg_ref,
                  cbuf, kprev, vprev, conv_scr, o_scr):
    s = pl.program_id(1)

    @pl.when(s == 0)
    def _():
        cbuf[0:CONV_HALO, :] = jnp.zeros((CONV_HALO, D_MODEL), F32)
        kprev[...] = jnp.zeros_like(kprev)
        vprev[...] = jnp.zeros_like(vprev)

    x = x_ref[...]
    ms = jnp.mean(x * x, axis=-1, keepdims=True)
    u = (x * lax.rsqrt(ms + RMS_EPS) * gmix_ref[...]).astype(BF16)

    def proj(lo, hi):
        return jnp.dot(u, win_ref[:, lo:hi], preferred_element_type=F32)

    c = proj(COL_VAL, COL_GATE) * _sigmoid(proj(COL_GATE, COL_Q))
    cbuf[CONV_HALO:CONV_HALO + TQ, :] = c
    n_buf = TQ + CONV_HALO
    nt = (((1,), (1,)), ((), ()))
    tn = (((0,), (0,)), ((), ()))
    qkv_rows = (COL_G - COL_Q) // CHUNKS
    gate_cols = (IN_COLS - COL_G) // CHUNKS
    qkv_parts, gate_parts = [], []
    for lt in range(CHUNKS):
        lanes = slice(lt * LANES, (lt + 1) * LANES)
        col = cbuf[:, lanes]
        shifted = [col] + [pltpu.roll(col, n_buf - r, 0) for r in range(1, SUBLANES)]
        acc = jnp.zeros((TQ, LANES), F32)
        for j in range(CONV_WIDTH):
            blk, r = divmod(CONV_HALO - (CONV_WIDTH - 1) + j, SUBLANES)
            acc = acc + wdw_ref[j:j + 1, lanes] * shifted[r][blk * SUBLANES:blk * SUBLANES + TQ]
        conv_scr[:, lanes] = acc
        rows = slice(lt * qkv_rows, (lt + 1) * qkv_rows)
        qkv_parts.append(lax.dot_general(wqkvt_ref[rows, :], u, nt, preferred_element_type=F32))
        gate_parts.append(proj(COL_G + lt * gate_cols, COL_G + (lt + 1) * gate_cols))
    cbuf[0:CONV_HALO, :] = cbuf[TQ:TQ + CONV_HALO, :]
    cv = conv_scr[...] + bdw_ref[...]
    mu = jnp.mean(cv, axis=-1, keepdims=True)
    cc = cv - mu
    var = jnp.mean(cc * cc, axis=-1, keepdims=True)
    cn = cc * lax.rsqrt(var + LN_EPS) * gln_ref[...] + bln_ref[...]
    cn = cn * _sigmoid(cn)
    y_conv = jnp.dot(cn.astype(BF16), wco_ref[...], preferred_element_type=F32)

    qkv_t = jnp.concatenate(qkv_parts, axis=0)
    half = ROT_DIM // 2
    pair = lax.broadcasted_iota(jnp.int32, (half, TQ), 0).astype(F32) * 2.0
    inv_freq = jnp.power(jnp.float32(ROPE_THETA), -pair / ROT_DIM)
    ang = pos_ref[...].astype(F32) * inv_freq
    cos = jnp.cos(ang)
    sin = jnp.sin(ang)

    def rotary(t):
        x1, x2 = t[0:half], t[half:ROT_DIM]
        return jnp.concatenate([x1 * cos - x2 * sin, x2 * cos + x1 * sin, t[ROT_DIM:]], axis=0)

    def head_rows(h):
        return slice(h * HEAD_DIM, (h + 1) * HEAD_DIM)

    k_row0 = N_Q_HEADS * HEAD_DIM
    v_row0 = k_row0 + N_KV_HEADS * HEAD_DIM
    q_t = jnp.concatenate([rotary(qkv_t[head_rows(h)]) for h in range(N_Q_HEADS)],
                          axis=0).astype(BF16)
    scale = HEAD_DIM ** -0.5
    k_t = jnp.concatenate(
        [rotary(qkv_t[k_row0 + g * HEAD_DIM:k_row0 + (g + 1) * HEAD_DIM]) * scale
         for g in range(N_KV_HEADS)], axis=0).astype(BF16)
    v_t = qkv_t[v_row0:v_row0 + N_KV_HEADS * HEAD_DIM].astype(BF16)

    group = N_Q_HEADS // N_KV_HEADS
    ki = lax.broadcasted_iota(jnp.int32, (2 * WINDOW, group * WINDOW), 0)
    qi = lax.broadcasted_iota(jnp.int32, (2 * WINDOW, group * WINDOW), 1) % WINDOW
    band = (ki > qi) & (ki <= qi + WINDOW)
    for qb in range(TQ // WINDOW):
        cols = slice(qb * WINDOW, (qb + 1) * WINDOW)
        if qb == 0:
            k_prev, v_prev = kprev[...], vprev[...]
            mask = band & (ki >= jnp.where(s > 0, 0, WINDOW))
        else:
            prev = slice((qb - 1) * WINDOW, qb * WINDOW)
            k_prev, v_prev = k_t[:, prev], v_t[:, prev]
            mask = band
        for g in range(N_KV_HEADS):
            k_cat = jnp.concatenate([k_prev[head_rows(g)], k_t[head_rows(g), cols]], axis=1)
            v_cat = jnp.concatenate([v_prev[head_rows(g)], v_t[head_rows(g), cols]], axis=1)
            q_blk = jnp.concatenate(
                [q_t[head_rows(g * group + i), cols] for i in range(group)], axis=1)
            sc = lax.dot_general(k_cat, q_blk, tn, preferred_element_type=F32)
            sc = jnp.where(mask, sc, -jnp.inf)
            sink = jnp.concatenate(
                [jnp.full((1, WINDOW), sinks_ref[g * group + i], F32) for i in range(group)],
                axis=1)
            m = jnp.maximum(jnp.max(sc, axis=0, keepdims=True), sink)
            e = jnp.exp(sc - m)
            den = jnp.sum(e, axis=0, keepdims=True) + jnp.exp(sink - m)
            p_t = (e * (1.0 / den)).astype(BF16)
            o_t = jnp.dot(v_cat, p_t, preferred_element_type=F32)
            for i in range(group):
                o_scr[head_rows(g * group + i), cols] = (
                    o_t[:, i * WINDOW:(i + 1) * WINDOW].astype(BF16))
    last = slice(TQ - WINDOW, TQ)
    kprev[...] = k_t[:, last]
    vprev[...] = v_t[:, last]
    y_attn = lax.dot_general(o_scr[...], wao_ref[...], tn, preferred_element_type=F32)

    gates = _sigmoid(jnp.concatenate(gate_parts, axis=1))
    mix = gates[:, :D_MODEL] * y_conv + gates[:, D_MODEL:] * y_attn
    h = x + jnp.dot(mix.astype(BF16), wout_ref[...], preferred_element_type=F32)
    h_ref[...] = h

    ms2 = jnp.mean(h * h, axis=-1, keepdims=True)
    u2 = h * lax.rsqrt(ms2 + RMS_EPS) * gffn_ref[...]
    for cidx in range(CHUNKS):
        u2_ref[pl.ds(cidx, TQ, stride=CHUNKS), :] = u2[:, cidx * LANES:(cidx + 1) * LANES]
    u2_hi = u2.astype(BF16)
    u2_lo = (u2 - u2_hi.astype(F32)).astype(BF16)
    wrt = wrt_ref[...]
    wrt_hi = wrt.astype(BF16)
    wrt_lo = (wrt - wrt_hi.astype(F32)).astype(BF16)
    logits = (lax.dot_general(wrt_hi, u2_hi, nt, preferred_element_type=F32)
              + lax.dot_general(wrt_hi, u2_lo, nt, preferred_element_type=F32)
              + lax.dot_general(wrt_lo, u2_hi, nt, preferred_element_type=F32)
              + brt_ref[...])
    eidx = lax.broadcasted_iota(jnp.int32, (N_EXPERTS, TQ), 0)
    vals, idxs = [], []
    for _ in range(TOP_K):
        m = jnp.max(logits, axis=0, keepdims=True)
        sel = jnp.min(jnp.where(logits == m, eidx, N_EXPERTS), axis=0, keepdims=True)
        vals.append(m)
        idxs.append(sel)
        logits = jnp.where(eidx == sel, -jnp.inf, logits)
    ex = [jnp.exp(v - vals[0]) for v in vals]
    den = ex[0] + ex[1] + ex[2] + ex[3]
    topi_ref[...] = jnp.concatenate(idxs, axis=0)
    topg_ref[...] = jnp.concatenate([e / den for e in ex], axis=0)


def _const_spec(shape):
    return pl.BlockSpec(shape, lambda b, s: (0,) * len(shape))


def _mixer(x, pos3, gmix, win, wqkvt, wdw, bdw, gln, bln, wco, sinks, wao, wout, gffn, wrt, brt):
    B, S, D = x.shape
    T = B * S
    n_s = S // TQ
    row_map = lambda b, s: (b * n_s + s, 0)
    return pl.pallas_call(
        _mixer_kernel,
        grid=(B, n_s),
        in_specs=[
            pl.BlockSpec((None, TQ, D), lambda b, s: (b, s, 0)),
            pl.BlockSpec((None, 1, TQ), lambda b, s: (b, 0, s)),
            _const_spec((1, D)),
            _const_spec((D, IN_COLS)),
            _const_spec((COL_G - COL_Q, D)),
            _const_spec((CONV_WIDTH, D)),
            _const_spec((1, D)),
            _const_spec((1, D)),
            _const_spec((1, D)),
            _const_spec((D, D)),
            pl.BlockSpec(memory_space=pltpu.SMEM),
            _const_spec((D, D)),
            _const_spec((D, D)),
            _const_spec((1, D)),
            _const_spec((N_EXPERTS, D)),
            _const_spec((N_EXPERTS, 1)),
        ],
        out_specs=[
            pl.BlockSpec((TQ, D), row_map),
            pl.BlockSpec((TQ * CHUNKS, LANES), row_map),
            pl.BlockSpec((TOP_K, TQ), lambda b, s: (0, b * n_s + s)),
            pl.BlockSpec((TOP_K, TQ), lambda b, s: (0, b * n_s + s)),
        ],
        out_shape=[
            jax.ShapeDtypeStruct((T, D), F32),
            jax.ShapeDtypeStruct((T * CHUNKS, LANES), F32),
            jax.ShapeDtypeStruct((TOP_K, T), jnp.int32),
            jax.ShapeDtypeStruct((TOP_K, T), F32),
        ],
        scratch_shapes=[
            pltpu.VMEM((TQ + CONV_HALO, D), F32),
            pltpu.VMEM((WINDOW, LANES), BF16),
            pltpu.VMEM((WINDOW, LANES), BF16),
            pltpu.VMEM((TQ, D), F32),
            pltpu.VMEM((D, TQ), BF16),
        ],
        compiler_params=pltpu.CompilerParams(
            dimension_semantics=("arbitrary", "arbitrary"),
            vmem_limit_bytes=VMEM_LIMIT),
        name="mixer",
    )(x, pos3, gmix, win, wqkvt, wdw, bdw, gln, bln, wco, sinks, wao, wout, gffn, wrt, brt)


def _route_kernel(topi_ref, dest_ref, blk_ref, ends_ref, cnt, base, *, n_blk_pad):
    phase = pl.program_id(0)
    step = pl.program_id(1)
    n_sub = ROUTE_TOK // ROUTE_SUB
    eidx = lax.broadcasted_iota(jnp.int32, (N_EXPERTS, ROUTE_SUB), 0)

    def onehot(j, sub):
        idx = topi_ref[j:j + 1, sub * ROUTE_SUB:(sub + 1) * ROUTE_SUB]
        return eidx == idx

    @pl.when((phase == 0) & (step == 0))
    def _():
        cnt[...] = jnp.zeros_like(cnt)

    @pl.when(phase == 0)
    def _():
        tot = jnp.zeros((N_EXPERTS, 1), F32)
        for sub in range(n_sub):
            for j in range(TOP_K):
                tot = tot + jnp.sum(onehot(j, sub).astype(F32), axis=1, keepdims=True)
        cnt[...] = cnt[...] + tot

    @pl.when((phase == 1) & (step == 0))
    def _():
        counts = cnt[...]
        padded = jnp.floor((counts + (BM - 1)) * (1.0 / BM)) * BM
        sub_i = lax.broadcasted_iota(jnp.int32, (N_EXPERTS, LANES), 0)
        lane_i = lax.broadcasted_iota(jnp.int32, (N_EXPERTS, LANES), 1)
        row = jnp.sum(jnp.where(sub_i == lane_i, padded, 0.0), axis=0, keepdims=True)
        ends = jnp.sum(jnp.where(lane_i <= sub_i, row, 0.0), axis=1, keepdims=True)
        base[...] = jnp.broadcast_to(ends, (N_EXPERTS, LANES)) - padded
        ends_row = jnp.sum(jnp.where(sub_i == lane_i, ends, 0.0), axis=0, keepdims=True)
        ends_ref[...] = ends_row.astype(jnp.int32)
        blk_start = (lax.broadcasted_iota(jnp.int32, (N_EXPERTS, n_blk_pad), 1) * BM).astype(F32)
        blk = jnp.sum((blk_start >= ends).astype(jnp.int32), axis=0, keepdims=True)
        blk_ref[...] = jnp.minimum(blk, N_EXPERTS - 1)

    @pl.when(phase == 1)
    def _():
        r = lax.broadcasted_iota(jnp.int32, (ROUTE_SUB, ROUTE_SUB), 0)
        cidx = lax.broadcasted_iota(jnp.int32, (ROUTE_SUB, ROUTE_SUB), 1)
        upper = jnp.where(r < cidx, 1.0, 0.0).astype(BF16)
        run = base[...][:, 0:1]
        for sub in range(n_sub):
            for j in range(TOP_K):
                oh = onehot(j, sub)
                ohf = oh.astype(F32)
                pre = jnp.dot(ohf.astype(BF16), upper, preferred_element_type=F32)
                rank = jnp.sum(jnp.where(oh, pre + run, 0.0), axis=0, keepdims=True)
                dest_ref[j:j + 1, sub * ROUTE_SUB:(sub + 1) * ROUTE_SUB] = rank.astype(jnp.int32)
                run = run + jnp.sum(ohf, axis=1, keepdims=True)
        base[...] = jnp.broadcast_to(run, (N_EXPERTS, LANES))


def _route(topi, n_blk_pad):
    T = topi.shape[1]
    n_steps = T // ROUTE_TOK
    return pl.pallas_call(
        functools.partial(_route_kernel, n_blk_pad=n_blk_pad),
        grid=(2, n_steps),
        in_specs=[pl.BlockSpec((TOP_K, ROUTE_TOK), lambda p, c: (0, c))],
        out_specs=[
            pl.BlockSpec((TOP_K, ROUTE_TOK), lambda p, c: (0, c * p)),
            pl.BlockSpec((1, n_blk_pad), lambda p, c: (0, 0)),
            pl.BlockSpec((1, LANES), lambda p, c: (0, 0)),
        ],
        out_shape=[
            jax.ShapeDtypeStruct((TOP_K, T), jnp.int32),
            jax.ShapeDtypeStruct((1, n_blk_pad), jnp.int32),
            jax.ShapeDtypeStruct((1, LANES), jnp.int32),
        ],
        scratch_shapes=[
            pltpu.VMEM((N_EXPERTS, LANES), F32),
            pltpu.VMEM((N_EXPERTS, LANES), F32),
        ],
        compiler_params=pltpu.CompilerParams(
            dimension_semantics=("arbitrary", "arbitrary")),
        name="route",
    )(topi)


def _slab(ref, row):
    return ref.at[pl.ds(pl.multiple_of(row * CHUNKS, CHUNKS), CHUNKS), :]


def _dispatch_kernel(dest_ref, ends_ref, u2_ref, xs_hbm, zbuf, zsem, sem):
    step = pl.program_id(0)

    @pl.when(step == 0)
    def _():
        zbuf[...] = jnp.zeros_like(zbuf)

        def zero_copy(e):
            start = pl.multiple_of((ends_ref[0, e] - BM) * CHUNKS, BM * CHUNKS)
            return pltpu.make_async_copy(zbuf, xs_hbm.at[pl.ds(start, BM * CHUNKS), :], zsem)

        def nonempty(e):
            prev_end = jnp.where(e > 0, ends_ref[0, jnp.maximum(e - 1, 0)], 0)
            return ends_ref[0, e] > prev_end

        def start_body(e, carry):
            @pl.when(nonempty(e))
            def _():
                zero_copy(e).start()
            return carry

        def wait_body(e, carry):
            @pl.when(nonempty(e))
            def _():
                zero_copy(e).wait()
            return carry

        lax.fori_loop(0, N_EXPERTS, start_body, 0)
        lax.fori_loop(0, N_EXPERTS, wait_body, 0)

        def tail_copy(b):
            start = pl.multiple_of(b * (BM * CHUNKS), BM * CHUNKS)
            return pltpu.make_async_copy(zbuf, xs_hbm.at[pl.ds(start, BM * CHUNKS), :], zsem)

        def tail_start(b, carry):
            tail_copy(b).start()
            return carry

        def tail_wait(b, carry):
            tail_copy(b).wait()
            return carry

        n_used = ends_ref[0, N_EXPERTS - 1] // BM
        n_blk = xs_hbm.shape[0] // (BM * CHUNKS)
        lax.fori_loop(n_used, n_blk, tail_start, 0)
        lax.fori_loop(n_used, n_blk, tail_wait, 0)

    def row_copy(t, j):
        return pltpu.make_async_copy(_slab(u2_ref, t), _slab(xs_hbm, dest_ref[j, t]), sem)

    def start_body(t, carry):
        for j in range(TOP_K):
            row_copy(t, j).start(priority=j % 2)
        return carry

    def wait_body(t, carry):
        for j in range(TOP_K):
            row_copy(t, j).wait()
        return carry

    lax.fori_loop(0, DISPATCH_TOK, start_body, 0)
    lax.fori_loop(0, DISPATCH_TOK, wait_body, 0)


def _dispatch(dest, ends, u2s, n_rows):
    T = dest.shape[1]
    return pl.pallas_call(
        _dispatch_kernel,
        grid=(T // DISPATCH_TOK,),
        in_specs=[
            pl.BlockSpec((TOP_K, DISPATCH_TOK), lambda i: (0, i), memory_space=pltpu.SMEM),
            pl.BlockSpec(memory_space=pltpu.SMEM),
            pl.BlockSpec((DISPATCH_TOK * CHUNKS, LANES), lambda i: (i, 0)),
        ],
        out_specs=pl.BlockSpec(memory_space=pl.ANY),
        out_shape=jax.ShapeDtypeStruct((n_rows * CHUNKS, LANES), F32),
        scratch_shapes=[
            pltpu.VMEM((BM * CHUNKS, LANES), F32),
            pltpu.SemaphoreType.DMA,
            pltpu.SemaphoreType.DMA,
        ],
        compiler_params=pltpu.CompilerParams(dimension_semantics=("arbitrary",)),
        name="dispatch",
    )(dest, ends, u2s)


def _expert_kernel(blk_ref, ends_ref, xs_ref, w1_ref, b1_ref, w2_ref, b2_ref, ys_ref,
                   w1_bf, w2_bf):
    i = pl.program_id(0)
    n_used = ends_ref[0, N_EXPERTS - 1] // BM

    new_expert = (i == 0) | (blk_ref[0, i] != blk_ref[0, jnp.maximum(i - 1, 0)])

    @pl.when((i < n_used) & new_expert)
    def _():
        w1_bf[...] = w1_ref[...].astype(BF16)
        w2_bf[...] = w2_ref[...].astype(BF16)

    @pl.when(i < n_used)
    def _():
        for sub in range(BM // EXPERT_SUB):
            r0 = sub * EXPERT_SUB * CHUNKS
            xb = jnp.concatenate(
                [xs_ref[pl.ds(r0 + cidx, EXPERT_SUB, stride=CHUNKS), :] for cidx in range(CHUNKS)],
                axis=1).astype(BF16)
            h1 = jnp.dot(xb, w1_bf[...], preferred_element_type=F32) + b1_ref[...]
            f = h1.shape[1] // 2
            glu = jnp.minimum(h1[:, :f], SWIGLU_LIMIT)
            lin = jnp.clip(h1[:, f:], -SWIGLU_LIMIT, SWIGLU_LIMIT)
            act = glu * _sigmoid(SWIGLU_ALPHA * glu) * (lin + 1.0)
            y = jnp.dot(act.astype(BF16), w2_bf[...], preferred_element_type=F32) + b2_ref[...]
            for cidx in range(CHUNKS):
                ys_ref[pl.ds(r0 + cidx, EXPERT_SUB, stride=CHUNKS), :] = (
                    y[:, cidx * LANES:(cidx + 1) * LANES])

    @pl.when(i >= n_used)
    def _():
        ys_ref[...] = jnp.zeros_like(ys_ref)


def _experts(blk_e, ends, xs, w1, b1, w2, b2, n_blk):
    E, D, F2 = w1.shape

    def used(i, ends_ref):
        return jnp.minimum(i, ends_ref[0, N_EXPERTS - 1] // BM - 1)

    grid_spec = pltpu.PrefetchScalarGridSpec(
        num_scalar_prefetch=2,
        grid=(n_blk,),
        in_specs=[
            pl.BlockSpec((BM * CHUNKS, LANES), lambda i, blk, ends: (used(i, ends), 0)),
            pl.BlockSpec((None, D, F2), lambda i, blk, ends: (blk[0, used(i, ends)], 0, 0)),
            pl.BlockSpec((None, 1, F2), lambda i, blk, ends: (blk[0, used(i, ends)], 0, 0)),
            pl.BlockSpec((None, F2 // 2, D), lambda i, blk, ends: (blk[0, used(i, ends)], 0, 0)),
            pl.BlockSpec((None, 1, D), lambda i, blk, ends: (blk[0, used(i, ends)], 0, 0)),
        ],
        out_specs=pl.BlockSpec((BM * CHUNKS, LANES), lambda i, blk, ends: (i, 0)),
        scratch_shapes=[
            pltpu.VMEM((D, F2), BF16),
            pltpu.VMEM((F2 // 2, D), BF16),
        ],
    )
    return pl.pallas_call(
        _expert_kernel,
        grid_spec=grid_spec,
        out_shape=jax.ShapeDtypeStruct((n_blk * BM * CHUNKS, LANES), F32),
        compiler_params=pltpu.CompilerParams(
            dimension_semantics=("arbitrary",), vmem_limit_bytes=VMEM_LIMIT),
        name="experts",
    )(blk_e, ends, xs, w1, b1, w2, b2)


def _combine_kernel(dest_ref, dest_next_ref, gate_ref, h_ref, gfin_ref, ys_hbm, out_ref,
                    gbuf0, gbuf1, sem0, sem1):
    k = pl.program_id(0)
    tok = COMBINE_TOK

    def issue(d_ref, off, gbuf, sem):
        def body(t, carry):
            for j in range(TOP_K):
                pltpu.make_async_copy(_slab(ys_hbm, d_ref[j, off + t]), _slab(gbuf, j * tok + t),
                                      sem).start(priority=j % 2)
            return carry
        lax.fori_loop(0, tok, body, 0)

    def drain(gbuf, sem):
        def body(t, carry):
            for j in range(TOP_K):
                pltpu.make_async_copy(_slab(ys_hbm, 0), _slab(gbuf, j * tok + t), sem).wait()
            return carry
        lax.fori_loop(0, tok, body, 0)

    def finish(gbuf, off):
        rows = slice(off, off + tok)
        gate = gate_ref[rows, :]
        parts = []
        ss = jnp.zeros((tok, 1), F32)
        for cidx in range(CHUNKS):
            acc = h_ref[rows, cidx * LANES:(cidx + 1) * LANES]
            for j in range(TOP_K):
                acc = acc + gate[:, j:j + 1] * gbuf[pl.ds(j * tok * CHUNKS + cidx, tok,
                                                          stride=CHUNKS), :]
            parts.append(acc)
            ss = ss + jnp.sum(acc * acc, axis=-1, keepdims=True)
        scale = lax.rsqrt(ss * (1.0 / D_MODEL) + RMS_EPS)
        for cidx in range(CHUNKS):
            lanes = slice(cidx * LANES, (cidx + 1) * LANES)
            out_ref[rows, lanes] = parts[cidx] * scale * gfin_ref[:, lanes]

    @pl.when(k == 0)
    def _():
        issue(dest_ref, 0, gbuf0, sem0)

    issue(dest_ref, tok, gbuf1, sem1)
    drain(gbuf0, sem0)
    finish(gbuf0, 0)

    @pl.when(k + 1 < pl.num_programs(0))
    def _():
        issue(dest_next_ref, 0, gbuf0, sem0)

    drain(gbuf1, sem1)
    finish(gbuf1, tok)


def _combine(dest, gate_t, h, gfin, ys):
    T, D = h.shape
    step_tok = 2 * COMBINE_TOK
    n_steps = T // step_tok
    gbuf = pltpu.VMEM((TOP_K * COMBINE_TOK * CHUNKS, LANES), F32)
    return pl.pallas_call(
        _combine_kernel,
        grid=(n_steps,),
        in_specs=[
            pl.BlockSpec((TOP_K, step_tok), lambda i: (0, i), memory_space=pltpu.SMEM),
            pl.BlockSpec((TOP_K, step_tok), lambda i: (0, jnp.minimum(i + 1, n_steps - 1)),
                         memory_space=pltpu.SMEM),
            pl.BlockSpec((step_tok, TOP_K), lambda i: (i, 0)),
            pl.BlockSpec((step_tok, D), lambda i: (i, 0)),
            pl.BlockSpec((1, D), lambda i: (0, 0)),
            pl.BlockSpec(memory_space=pl.ANY),
        ],
        out_specs=pl.BlockSpec((step_tok, D), lambda i: (i, 0)),
        out_shape=jax.ShapeDtypeStruct((T, D), F32),
        scratch_shapes=[gbuf, gbuf, pltpu.SemaphoreType.DMA, pltpu.SemaphoreType.DMA],
        compiler_params=pltpu.CompilerParams(dimension_semantics=("arbitrary",)),
        name="combine",
    )(dest, dest, gate_t, h, gfin, ys)


def kernel(x, positions, g_mix_norm, w_in, w_dw, b_dw, g_conv_ln, b_conv_ln, w_conv_out,
           attn_sinks, w_attn_out, w_out, g_ffn_norm, w_router, b_router,
           w_mlp1, b_mlp1, w_mlp2, b_mlp2, g_final):
    B, S, D = x.shape
    T = B * S
    depth = g_mix_norm.shape[0]
    assert depth == 1 and D == D_MODEL and S % TQ == 0 and T % ROUTE_TOK == 0
    assert T % DISPATCH_TOK == 0 and T % (2 * COMBINE_TOK) == 0
    l = 0
    n_blk = (T * TOP_K) // BM + N_EXPERTS
    n_blk_pad = -(-n_blk // LANES) * LANES

    h, u2s, topi, topg = _mixer(
        x, positions[:, None, :], g_mix_norm[l][None, :],
        w_in[l].astype(BF16), w_in[l][:, COL_Q:COL_G].T.astype(BF16), w_dw[l],
        b_dw[l][None, :], g_conv_ln[l][None, :], b_conv_ln[l][None, :],
        w_conv_out[l].astype(BF16), attn_sinks[l],
        w_attn_out[l].astype(BF16), w_out[l].astype(BF16), g_ffn_norm[l][None, :],
        w_router[l].T, b_router[l][:, None])
    dest, blk_e, ends = _route(topi, n_blk_pad)
    xs = _dispatch(dest, ends, u2s, n_blk * BM)
    ys = _experts(blk_e, ends, xs, w_mlp1[l], b_mlp1[l][:, None, :],
                  w_mlp2[l], b_mlp2[l][:, None, :], n_blk)
    out = _combine(dest, topg.T, h, g_final[None, :], ys)
    return out.reshape(B, S, D)
```

```python
import functools

import jax
import jax.numpy as jnp
import numpy as np
from jax import lax
from jax.experimental import pallas as pl
from jax.experimental.pallas import tpu as pltpu

F32 = jnp.float32
BF16 = jnp.bfloat16

D_MODEL = 1024
CONV_WIDTH = 31
HEAD_DIM = 64
N_Q_HEADS = 16
N_KV_HEADS = 2
WINDOW = 128
ROT_DIM = 16
ROPE_THETA = 500000.0
N_EXPERTS = 32
TOP_K = 4
SWIGLU_LIMIT = 7.0
SWIGLU_ALPHA = 1.702
RMS_EPS = 1e-5
LN_EPS = 1e-5

LANES = 128
SUBLANES = 8
CHUNKS = D_MODEL // LANES

COL_VAL = 0
COL_GATE = D_MODEL
COL_Q = 2 * D_MODEL
COL_K = COL_Q + N_Q_HEADS * HEAD_DIM
COL_V = COL_K + N_KV_HEADS * HEAD_DIM
COL_G = COL_V + N_KV_HEADS * HEAD_DIM
IN_COLS = COL_G + 2 * D_MODEL

TQ = 256
CONV_HALO = 32
GLU_TILES = 2
ROUTE_TOK = 2048
ROUTE_SUB = 512
BM = 256
DISPATCH_TOK = 512
COMBINE_TOK = 256
VMEM_LIMIT = 56 * 1024 * 1024


def _sigmoid(v):
    return 1.0 / (1.0 + jnp.exp(-v))


def _mixer_kernel(x_ref, pos_ref, gmix_ref, win_ref, wqkvt_ref, wdw_ref, bdw_ref, gln_ref,
                  bln_ref, wco_ref, sinks_ref, wao_ref, wout_ref, gffn_ref, wrt_ref, brt_ref,
                  h_ref, u2_ref, topi_ref, topg_ref,
                  cbuf, kprev, vprev, conv_scr, o_scr):
    s = pl.program_id(1)

    @pl.when(s == 0)
    def _():
        cbuf[0:CONV_HALO, :] = jnp.zeros((CONV_HALO, D_MODEL), F32)
        kprev[...] = jnp.zeros_like(kprev)
        vprev[...] = jnp.zeros_like(vprev)

    x = x_ref[...]
    ms = jnp.mean(x * x, axis=-1, keepdims=True)
    u = (x * lax.rsqrt(ms + RMS_EPS) * gmix_ref[...]).astype(BF16)

    def proj(lo, hi):
        return jnp.dot(u, win_ref[:, lo:hi], preferred_element_type=F32)

    n_buf = TQ + CONV_HALO
    nt = (((1,), (1,)), ((), ()))
    tn = (((0,), (0,)), ((), ()))
    qkv_rows = (COL_G - COL_Q) // CHUNKS
    gate_cols = (IN_COLS - COL_G) // CHUNKS
    qkv_parts, gate_parts = [], []
    for lt in range(CHUNKS):
        lanes = slice(lt * LANES, (lt + 1) * LANES)
        if lt % GLU_TILES == 0:
            ch = slice(lt * LANES, (lt + GLU_TILES) * LANES)
            cbuf[CONV_HALO:CONV_HALO + TQ, ch] = (
                proj(COL_VAL + ch.start, COL_VAL + ch.stop)
                * _sigmoid(proj(COL_GATE + ch.start, COL_GATE + ch.stop)))
        col = cbuf[:, lanes]
        shifted = [col] + [pltpu.roll(col, n_buf - r, 0) for r in range(1, SUBLANES)]
        acc = jnp.zeros((TQ, LANES), F32)
        for j in range(CONV_WIDTH):
            blk, r = divmod(CONV_HALO - (CONV_WIDTH - 1) + j, SUBLANES)
            acc = acc + wdw_ref[j:j + 1, lanes] * shifted[r][blk * SUBLANES:blk * SUBLANES + TQ]
        conv_scr[:, lanes] = acc
        rows = slice(lt * qkv_rows, (lt + 1) * qkv_rows)
        qkv_parts.append(lax.dot_general(wqkvt_ref[rows, :], u, nt, preferred_element_type=F32))
        gate_parts.append(proj(COL_G + lt * gate_cols, COL_G + (lt + 1) * gate_cols))
    cbuf[0:CONV_HALO, :] = cbuf[TQ:TQ + CONV_HALO, :]
    cv = conv_scr[...] + bdw_ref[...]
    mu = jnp.mean(cv, axis=-1, keepdims=True)
    cc = cv - mu
    var = jnp.mean(cc * cc, axis=-1, keepdims=True)
    cn = cc * lax.rsqrt(var + LN_EPS) * gln_ref[...] + bln_ref[...]
    cn = cn * _sigmoid(cn)
    y_conv = jnp.dot(cn.astype(BF16), wco_ref[...], preferred_element_type=F32)

    qkv_t = jnp.concatenate(qkv_parts, axis=0)
    half = ROT_DIM // 2
    pair = lax.broadcasted_iota(jnp.int32, (half, TQ), 0).astype(F32) * 2.0
    inv_freq = jnp.power(jnp.float32(ROPE_THETA), -pair / ROT_DIM)
    ang = pos_ref[...].astype(F32) * inv_freq
    cos = jnp.cos(ang)
    sin = jnp.sin(ang)

    def rotary(t):
        x1, x2 = t[0:half], t[half:ROT_DIM]
        return jnp.concatenate([x1 * cos - x2 * sin, x2 * cos + x1 * sin, t[ROT_DIM:]], axis=0)

    def head_rows(h):
        return slice(h * HEAD_DIM, (h + 1) * HEAD_DIM)

    k_row0 = N_Q_HEADS * HEAD_DIM
    v_row0 = k_row0 + N_KV_HEADS * HEAD_DIM
    q_t = jnp.concatenate([rotary(qkv_t[head_rows(h)]) for h in range(N_Q_HEADS)],
                          axis=0).astype(BF16)
    scale = HEAD_DIM ** -0.5
    k_t = jnp.concatenate(
        [rotary(qkv_t[k_row0 + g * HEAD_DIM:k_row0 + (g + 1) * HEAD_DIM]) * scale
         for g in range(N_KV_HEADS)], axis=0).astype(BF16)
    v_t = qkv_t[v_row0:v_row0 + N_KV_HEADS * HEAD_DIM].astype(BF16)

    group = N_Q_HEADS // N_KV_HEADS
    ki = lax.broadcasted_iota(jnp.int32, (2 * WINDOW, group * WINDOW), 0)
    qi = lax.broadcasted_iota(jnp.int32, (2 * WINDOW, group * WINDOW), 1) % WINDOW
    band = (ki > qi) & (ki <= qi + WINDOW)
    for qb in range(TQ // WINDOW):
        cols = slice(qb * WINDOW, (qb + 1) * WINDOW)
        if qb == 0:
            k_prev, v_prev = kprev[...], vprev[...]
            mask = band & (ki >= jnp.where(s > 0, 0, WINDOW))
        else:
            prev = slice((qb - 1) * WINDOW, qb * WINDOW)
            k_prev, v_prev = k_t[:, prev], v_t[:, prev]
            mask = band
        for g in range(N_KV_HEADS):
            k_cat = jnp.concatenate([k_prev[head_rows(g)], k_t[head_rows(g), cols]], axis=1)
            v_cat = jnp.concatenate([v_prev[head_rows(g)], v_t[head_rows(g), cols]], axis=1)
            q_blk = jnp.concatenate(
                [q_t[head_rows(g * group + i), cols] for i in range(group)], axis=1)
            sc = lax.dot_general(k_cat, q_blk, tn, preferred_element_type=F32)
            sc = jnp.where(mask, sc, -jnp.inf)
            sink = jnp.concatenate(
                [jnp.full((1, WINDOW), sinks_ref[g * group + i], F32) for i in range(group)],
                axis=1)
            m = jnp.maximum(jnp.max(sc, axis=0, keepdims=True), sink)
            e = jnp.exp(sc - m)
            den = jnp.sum(e, axis=0, keepdims=True) + jnp.exp(sink - m)
            p_t = (e * (1.0 / den)).astype(BF16)
            o_t = jnp.dot(v_cat, p_t, preferred_element_type=F32)
            for i in range(group):
                o_scr[head_rows(g * group + i), cols] = (
                    o_t[:, i * WINDOW:(i + 1) * WINDOW].astype(BF16))
    last = slice(TQ - WINDOW, TQ)
    kprev[...] = k_t[:, last]
    vprev[...] = v_t[:, last]
    y_attn = lax.dot_general(o_scr[...], wao_ref[...], tn, preferred_element_type=F32)

    gates = _sigmoid(jnp.concatenate(gate_parts, axis=1))
    mix = gates[:, :D_MODEL] * y_conv + gates[:, D_MODEL:] * y_attn
    h = x + jnp.dot(mix.astype(BF16), wout_ref[...], preferred_element_type=F32)
    h_ref[...] = h

    ms2 = jnp.mean(h * h, axis=-1, keepdims=True)
    u2 = h * lax.rsqrt(ms2 + RMS_EPS) * gffn_ref[...]
    for cidx in range(CHUNKS):
        u2_ref[pl.ds(cidx, TQ, stride=CHUNKS), :] = u2[:, cidx * LANES:(cidx + 1) * LANES]
    u2_hi = u2.astype(BF16)
    u2_lo = (u2 - u2_hi.astype(F32)).astype(BF16)
    wrt = wrt_ref[...]
    wrt_hi = wrt.astype(BF16)
    wrt_lo = (wrt - wrt_hi.astype(F32)).astype(BF16)
    logits = (lax.dot_general(wrt_hi, u2_hi, nt, preferred_element_type=F32)
              + lax.dot_general(wrt_hi, u2_lo, nt, preferred_element_type=F32)
              + lax.dot_general(wrt_lo, u2_hi, nt, preferred_element_type=F32)
              + brt_ref[...])
    eidx = lax.broadcasted_iota(jnp.int32, (N_EXPERTS, TQ), 0)
    vals, idxs = [], []
    for _ in range(TOP_K):
        m = jnp.max(logits, axis=0, keepdims=True)
        sel = jnp.min(jnp.where(logits == m, eidx, N_EXPERTS), axis=0, keepdims=True)
        vals.append(m)
        idxs.append(sel)
        logits = jnp.where(eidx == sel, -jnp.inf, logits)
    ex = [jnp.exp(v - vals[0]) for v in vals]
    den = ex[0] + ex[1] + ex[2] + ex[3]
    topi_ref[...] = jnp.concatenate(idxs, axis=0)
    topg_ref[...] = jnp.concatenate([e / den for e in ex], axis=0)


def _const_spec(shape):
    return pl.BlockSpec(shape, lambda b, s: (0,) * len(shape))


def _mixer(x, pos3, gmix, win, wqkvt, wdw, bdw, gln, bln, wco, sinks, wao, wout, gffn, wrt, brt):
    B, S, D = x.shape
    T = B * S
    n_s = S // TQ
    row_map = lambda b, s: (b * n_s + s, 0)
    return pl.pallas_call(
        _mixer_kernel,
        grid=(B, n_s),
        in_specs=[
            pl.BlockSpec((None, TQ, D), lambda b, s: (b, s, 0)),
            pl.BlockSpec((None, 1, TQ), lambda b, s: (b, 0, s)),
            _const_spec((1, D)),
            _const_spec((D, IN_COLS)),
            _const_spec((COL_G - COL_Q, D)),
            _const_spec((CONV_WIDTH, D)),
            _const_spec((1, D)),
            _const_spec((1, D)),
            _const_spec((1, D)),
            _const_spec((D, D)),
            pl.BlockSpec(memory_space=pltpu.SMEM),
            _const_spec((D, D)),
            _const_spec((D, D)),
            _const_spec((1, D)),
            _const_spec((N_EXPERTS, D)),
            _const_spec((N_EXPERTS, 1)),
        ],
        out_specs=[
            pl.BlockSpec((TQ, D), row_map),
            pl.BlockSpec((TQ * CHUNKS, LANES), row_map),
            pl.BlockSpec((TOP_K, TQ), lambda b, s: (0, b * n_s + s)),
            pl.BlockSpec((TOP_K, TQ), lambda b, s: (0, b * n_s + s)),
        ],
        out_shape=[
            jax.ShapeDtypeStruct((T, D), F32),
            jax.ShapeDtypeStruct((T * CHUNKS, LANES), F32),
            jax.ShapeDtypeStruct((TOP_K, T), jnp.int32),
            jax.ShapeDtypeStruct((TOP_K, T), F32),
        ],
        scratch_shapes=[
            pltpu.VMEM((TQ + CONV_HALO, D), F32),
            pltpu.VMEM((WINDOW, LANES), BF16),
            pltpu.VMEM((WINDOW, LANES), BF16),
            pltpu.VMEM((TQ, D), F32),
            pltpu.VMEM((D, TQ), BF16),
        ],
        compiler_params=pltpu.CompilerParams(
            dimension_semantics=("arbitrary", "arbitrary"),
            vmem_limit_bytes=VMEM_LIMIT),
        name="mixer",
    )(x, pos3, gmix, win, wqkvt, wdw, bdw, gln, bln, wco, sinks, wao, wout, gffn, wrt, brt)


def _route_kernel(topi_ref, dest_ref, blk_ref, ends_ref, cnt, base, *, n_blk_pad):
    phase = pl.program_id(0)
    step = pl.program_id(1)
    n_sub = ROUTE_TOK // ROUTE_SUB
    eidx = lax.broadcasted_iota(jnp.int32, (N_EXPERTS, ROUTE_SUB), 0)

    def onehot(j, sub):
        idx = topi_ref[j:j + 1, sub * ROUTE_SUB:(sub + 1) * ROUTE_SUB]
        return eidx == idx

    @pl.when((phase == 0) & (step == 0))
    def _():
        cnt[...] = jnp.zeros_like(cnt)

    @pl.when(phase == 0)
    def _():
        tot = jnp.zeros((N_EXPERTS, 1), F32)
        for sub in range(n_sub):
            for j in range(TOP_K):
                tot = tot + jnp.sum(onehot(j, sub).astype(F32), axis=1, keepdims=True)
        cnt[...] = cnt[...] + tot

    @pl.when((phase == 1) & (step == 0))
    def _():
        counts = cnt[...]
        padded = jnp.floor((counts + (BM - 1)) * (1.0 / BM)) * BM
        sub_i = lax.broadcasted_iota(jnp.int32, (N_EXPERTS, LANES), 0)
        lane_i = lax.broadcasted_iota(jnp.int32, (N_EXPERTS, LANES), 1)
        row = jnp.sum(jnp.where(sub_i == lane_i, padded, 0.0), axis=0, keepdims=True)
        ends = jnp.sum(jnp.where(lane_i <= sub_i, row, 0.0), axis=1, keepdims=True)
        base[...] = jnp.broadcast_to(ends, (N_EXPERTS, LANES)) - padded
        ends_row = jnp.sum(jnp.where(sub_i == lane_i, ends, 0.0), axis=0, keepdims=True)
        ends_ref[...] = ends_row.astype(jnp.int32)
        blk_start = (lax.broadcasted_iota(jnp.int32, (N_EXPERTS, n_blk_pad), 1) * BM).astype(F32)
        blk = jnp.sum((blk_start >= ends).astype(jnp.int32), axis=0, keepdims=True)
        blk_ref[...] = jnp.minimum(blk, N_EXPERTS - 1)

    @pl.when(phase == 1)
    def _():
        r = lax.broadcasted_iota(jnp.int32, (ROUTE_SUB, ROUTE_SUB), 0)
        cidx = lax.broadcasted_iota(jnp.int32, (ROUTE_SUB, ROUTE_SUB), 1)
        upper = jnp.where(r < cidx, 1.0, 0.0).astype(BF16)
        run = base[...][:, 0:1]
        for sub in range(n_sub):
            for j in range(TOP_K):
                oh = onehot(j, sub)
                ohf = oh.astype(F32)
                pre = jnp.dot(ohf.astype(BF16), upper, preferred_element_type=F32)
                rank = jnp.sum(jnp.where(oh, pre + run, 0.0), axis=0, keepdims=True)
                dest_ref[j:j + 1, sub * ROUTE_SUB:(sub + 1) * ROUTE_SUB] = rank.astype(jnp.int32)
                run = run + jnp.sum(ohf, axis=1, keepdims=True)
        base[...] = jnp.broadcast_to(run, (N_EXPERTS, LANES))


def _route(topi, n_blk_pad):
    T = topi.shape[1]
    n_steps = T // ROUTE_TOK
    return pl.pallas_call(
        functools.partial(_route_kernel, n_blk_pad=n_blk_pad),
        grid=(2, n_steps),
        in_specs=[pl.BlockSpec((TOP_K, ROUTE_TOK), lambda p, c: (0, c))],
        out_specs=[
            pl.BlockSpec((TOP_K, ROUTE_TOK), lambda p, c: (0, c * p)),
            pl.BlockSpec((1, n_blk_pad), lambda p, c: (0, 0)),
            pl.BlockSpec((1, LANES), lambda p, c: (0, 0)),
        ],
        out_shape=[
            jax.ShapeDtypeStruct((TOP_K, T), jnp.int32),
            jax.ShapeDtypeStruct((1, n_blk_pad), jnp.int32),
            jax.ShapeDtypeStruct((1, LANES), jnp.int32),
        ],
        scratch_shapes=[
            pltpu.VMEM((N_EXPERTS, LANES), F32),
            pltpu.VMEM((N_EXPERTS, LANES), F32),
        ],
        compiler_params=pltpu.CompilerParams(
            dimension_semantics=("arbitrary", "arbitrary")),
        name="route",
    )(topi)


def _slab(ref, row):
    return ref.at[pl.ds(pl.multiple_of(row * CHUNKS, CHUNKS), CHUNKS), :]


def _dispatch_kernel(dest_ref, ends_ref, u2_ref, xs_hbm, zbuf, zsem, sem):
    step = pl.program_id(0)

    @pl.when(step == 0)
    def _():
        zbuf[...] = jnp.zeros_like(zbuf)

        def zero_copy(e):
            start = pl.multiple_of((ends_ref[0, e] - BM) * CHUNKS, BM * CHUNKS)
            return pltpu.make_async_copy(zbuf, xs_hbm.at[pl.ds(start, BM * CHUNKS), :], zsem)

        def nonempty(e):
            prev_end = jnp.where(e > 0, ends_ref[0, jnp.maximum(e - 1, 0)], 0)
            return ends_ref[0, e] > prev_end

        def start_body(e, carry):
            @pl.when(nonempty(e))
            def _():
                zero_copy(e).start()
            return carry

        def wait_body(e, carry):
            @pl.when(nonempty(e))
            def _():
                zero_copy(e).wait()
            return carry

        lax.fori_loop(0, N_EXPERTS, start_body, 0)
        lax.fori_loop(0, N_EXPERTS, wait_body, 0)

        def tail_copy(b):
            start = pl.multiple_of(b * (BM * CHUNKS), BM * CHUNKS)
            return pltpu.make_async_copy(zbuf, xs_hbm.at[pl.ds(start, BM * CHUNKS), :], zsem)

        def tail_start(b, carry):
            tail_copy(b).start()
            return carry

        def tail_wait(b, carry):
            tail_copy(b).wait()
            return carry

        n_used = ends_ref[0, N_EXPERTS - 1] // BM
        n_blk = xs_hbm.shape[0] // (BM * CHUNKS)
        lax.fori_loop(n_used, n_blk, tail_start, 0)
        lax.fori_loop(n_used, n_blk, tail_wait, 0)

    def row_copy(t, j):
        return pltpu.make_async_copy(_slab(u2_ref, t), _slab(xs_hbm, dest_ref[j, t]), sem)

    def start_body(t, carry):
        for j in range(TOP_K):
            row_copy(t, j).start(priority=j % 2)
        return carry

    def wait_body(t, carry):
        for j in range(TOP_K):
            row_copy(t, j).wait()
        return carry

    lax.fori_loop(0, DISPATCH_TOK, start_body, 0)
    lax.fori_loop(0, DISPATCH_TOK, wait_body, 0)


def _dispatch(dest, ends, u2s, n_rows):
    T = dest.shape[1]
    return pl.pallas_call(
        _dispatch_kernel,
        grid=(T // DISPATCH_TOK,),
        in_specs=[
            pl.BlockSpec((TOP_K, DISPATCH_TOK), lambda i: (0, i), memory_space=pltpu.SMEM),
            pl.BlockSpec(memory_space=pltpu.SMEM),
            pl.BlockSpec((DISPATCH_TOK * CHUNKS, LANES), lambda i: (i, 0)),
        ],
        out_specs=pl.BlockSpec(memory_space=pl.ANY),
        out_shape=jax.ShapeDtypeStruct((n_rows * CHUNKS, LANES), F32),
        scratch_shapes=[
            pltpu.VMEM((BM * CHUNKS, LANES), F32),
            pltpu.SemaphoreType.DMA,
            pltpu.SemaphoreType.DMA,
        ],
        compiler_params=pltpu.CompilerParams(dimension_semantics=("arbitrary",)),
        name="dispatch",
    )(dest, ends, u2s)


def _expert_kernel(blk_ref, ends_ref, xs_ref, w1_ref, b1_ref, w2_ref, b2_ref, ys_ref,
                   w1_bf, w2_bf):
    i = pl.program_id(0)
    n_used = ends_ref[0, N_EXPERTS - 1] // BM

    new_expert = (i == 0) | (blk_ref[0, i] != blk_ref[0, jnp.maximum(i - 1, 0)])

    @pl.when((i < n_used) & new_expert)
    def _():
        w1_bf[...] = w1_ref[...].astype(BF16)
        w2_bf[...] = w2_ref[...].astype(BF16)

    @pl.when(i < n_used)
    def _():
        xb = jnp.concatenate(
            [xs_ref[pl.ds(cidx, BM, stride=CHUNKS), :] for cidx in range(CHUNKS)],
            axis=1).astype(BF16)
        h1 = jnp.dot(xb, w1_bf[...], preferred_element_type=F32) + b1_ref[...]
        f = h1.shape[1] // 2
        glu = jnp.minimum(h1[:, :f], SWIGLU_LIMIT)
        lin = jnp.clip(h1[:, f:], -SWIGLU_LIMIT, SWIGLU_LIMIT)
        act = glu * _sigmoid(SWIGLU_ALPHA * glu) * (lin + 1.0)
        y = jnp.dot(act.astype(BF16), w2_bf[...], preferred_element_type=F32) + b2_ref[...]
        for cidx in range(CHUNKS):
            ys_ref[pl.ds(cidx, BM, stride=CHUNKS), :] = y[:, cidx * LANES:(cidx + 1) * LANES]

    @pl.when(i >= n_used)
    def _():
        ys_ref[...] = jnp.zeros_like(ys_ref)


def _experts(blk_e, ends, xs, w1, b1, w2, b2, n_blk):
    E, D, F2 = w1.shape

    def used(i, ends_ref):
        return jnp.minimum(i, ends_ref[0, N_EXPERTS - 1] // BM - 1)

    grid_spec = pltpu.PrefetchScalarGridSpec(
        num_scalar_prefetch=2,
        grid=(n_blk,),
        in_specs=[
            pl.BlockSpec((BM * CHUNKS, LANES), lambda i, blk, ends: (used(i, ends), 0)),
            pl.BlockSpec((None, D, F2), lambda i, blk, ends: (blk[0, used(i, ends)], 0, 0)),
            pl.BlockSpec((None, 1, F2), lambda i, blk, ends: (blk[0, used(i, ends)], 0, 0)),
            pl.BlockSpec((None, F2 // 2, D), lambda i, blk, ends: (blk[0, used(i, ends)], 0, 0)),
            pl.BlockSpec((None, 1, D), lambda i, blk, ends: (blk[0, used(i, ends)], 0, 0)),
        ],
        out_specs=pl.BlockSpec((BM * CHUNKS, LANES), lambda i, blk, ends: (i, 0)),
        scratch_shapes=[
            pltpu.VMEM((D, F2), BF16),
            pltpu.VMEM((F2 // 2, D), BF16),
        ],
    )
    return pl.pallas_call(
        _expert_kernel,
        grid_spec=grid_spec,
        out_shape=jax.ShapeDtypeStruct((n_blk * BM * CHUNKS, LANES), F32),
        compiler_params=pltpu.CompilerParams(
            dimension_semantics=("arbitrary",), vmem_limit_bytes=VMEM_LIMIT),
        name="experts",
    )(blk_e, ends, xs, w1, b1, w2, b2)


def _combine_kernel(dest_ref, dest_next_ref, gate_ref, h_ref, gfin_ref, ys_hbm, out_ref,
                    gbuf0, gbuf1, sem0, sem1):
    k = pl.program_id(0)
    tok = COMBINE_TOK

    def issue(d_ref, off, gbuf, sem):
        def body(t, carry):
            for j in range(TOP_K):
                pltpu.make_async_copy(_slab(ys_hbm, d_ref[j, off + t]), _slab(gbuf, j * tok + t),
                                      sem).start(priority=j % 2)
            return carry
        lax.fori_loop(0, tok, body, 0)

    def drain(gbuf, sem):
        def body(t, carry):
            for j in range(TOP_K):
                pltpu.make_async_copy(_slab(ys_hbm, 0), _slab(gbuf, j * tok + t), sem).wait()
            return carry
        lax.fori_loop(0, tok, body, 0)

    def finish(gbuf, off):
        rows = slice(off, off + tok)
        gate = gate_ref[rows, :]
        parts = []
        ss = jnp.zeros((tok, 1), F32)
        for cidx in range(CHUNKS):
            acc = h_ref[rows, cidx * LANES:(cidx + 1) * LANES]
            for j in range(TOP_K):
                acc = acc + gate[:, j:j + 1] * gbuf[pl.ds(j * tok * CHUNKS + cidx, tok,
                                                          stride=CHUNKS), :]
            parts.append(acc)
            ss = ss + jnp.sum(acc * acc, axis=-1, keepdims=True)
        scale = lax.rsqrt(ss * (1.0 / D_MODEL) + RMS_EPS)
        for cidx in range(CHUNKS):
            lanes = slice(cidx * LANES, (cidx + 1) * LANES)
            out_ref[rows, lanes] = parts[cidx] * scale * gfin_ref[:, lanes]

    @pl.when(k == 0)
    def _():
        issue(dest_ref, 0, gbuf0, sem0)

    issue(dest_ref, tok, gbuf1, sem1)
    drain(gbuf0, sem0)
    finish(gbuf0, 0)

    @pl.when(k + 1 < pl.num_programs(0))
    def _():
        issue(dest_next_ref, 0, gbuf0, sem0)

    drain(gbuf1, sem1)
    finish(gbuf1, tok)


def _combine(dest, gate_t, h, gfin, ys):
    T, D = h.shape
    step_tok = 2 * COMBINE_TOK
    n_steps = T // step_tok
    gbuf = pltpu.VMEM((TOP_K * COMBINE_TOK * CHUNKS, LANES), F32)
    return pl.pallas_call(
        _combine_kernel,
        grid=(n_steps,),
        in_specs=[
            pl.BlockSpec((TOP_K, step_tok), lambda i: (0, i), memory_space=pltpu.SMEM),
            pl.BlockSpec((TOP_K, step_tok), lambda i: (0, jnp.minimum(i + 1, n_steps - 1)),
                         memory_space=pltpu.SMEM),
            pl.BlockSpec((step_tok, TOP_K), lambda i: (i, 0)),
            pl.BlockSpec((step_tok, D), lambda i: (i, 0)),
            pl.BlockSpec((1, D), lambda i: (0, 0)),
            pl.BlockSpec(memory_space=pl.ANY),
        ],
        out_specs=pl.BlockSpec((step_tok, D), lambda i: (i, 0)),
        out_shape=jax.ShapeDtypeStruct((T, D), F32),
        scratch_shapes=[gbuf, gbuf, pltpu.SemaphoreType.DMA, pltpu.SemaphoreType.DMA],
        compiler_params=pltpu.CompilerParams(dimension_semantics=("arbitrary",)),
        name="combine",
    )(dest, dest, gate_t, h, gfin, ys)


def kernel(x, positions, g_mix_norm, w_in, w_dw, b_dw, g_conv_ln, b_conv_ln, w_conv_out,
           attn_sinks, w_attn_out, w_out, g_ffn_norm, w_router, b_router,
           w_mlp1, b_mlp1, w_mlp2, b_mlp2, g_final):
    B, S, D = x.shape
    T = B * S
    depth = g_mix_norm.shape[0]
    assert depth == 1 and D == D_MODEL and S % TQ == 0 and T % ROUTE_TOK == 0
    assert T % DISPATCH_TOK == 0 and T % (2 * COMBINE_TOK) == 0
    l = 0
    n_blk = (T * TOP_K) // BM + N_EXPERTS
    n_blk_pad = -(-n_blk // LANES) * LANES

    h, u2s, topi, topg = _mixer(
        x, positions[:, None, :], g_mix_norm[l][None, :],
        w_in[l].astype(BF16), w_in[l][:, COL_Q:COL_G].T.astype(BF16), w_dw[l],
        b_dw[l][None, :], g_conv_ln[l][None, :], b_conv_ln[l][None, :],
        w_conv_out[l].astype(BF16), attn_sinks[l],
        w_attn_out[l].astype(BF16), w_out[l].astype(BF16), g_ffn_norm[l][None, :],
        w_router[l].T, b_router[l][:, None])
    dest, blk_e, ends = _route(topi, n_blk_pad)
    xs = _dispatch(dest, ends, u2s, n_blk * BM)
    ys = _experts(blk_e, ends, xs, w_mlp1[l], b_mlp1[l][:, None, :],
                  w_mlp2[l], b_mlp2[l][:, None, :], n_blk)
    out = _combine(dest, topg.T, h, g_final[None, :], ys)
    return out.reshape(B, S, D)
```

```python
import functools

import jax
import jax.numpy as jnp
import numpy as np
from jax import lax
from jax.experimental import pallas as pl
from jax.experimental.pallas import tpu as pltpu

F32 = jnp.float32
BF16 = jnp.bfloat16

D_MODEL = 1024
CONV_WIDTH = 31
HEAD_DIM = 64
N_Q_HEADS = 16
N_KV_HEADS = 2
WINDOW = 128
ROT_DIM = 16
ROPE_THETA = 500000.0
N_EXPERTS = 32
TOP_K = 4
SWIGLU_LIMIT = 7.0
SWIGLU_ALPHA = 1.702
RMS_EPS = 1e-5
LN_EPS = 1e-5

LANES = 128
SUBLANES = 8
CHUNKS = D_MODEL // LANES

COL_VAL = 0
COL_GATE = D_MODEL
COL_Q = 2 * D_MODEL
COL_K = COL_Q + N_Q_HEADS * HEAD_DIM
COL_V = COL_K + N_KV_HEADS * HEAD_DIM
COL_G = COL_V + N_KV_HEADS * HEAD_DIM
IN_COLS = COL_G + 2 * D_MODEL

TQ = 256
CONV_HALO = 32
GLU_TILES = 2
ROUTE_TOK = 2048
ROUTE_SUB = 512
BM = 256
DISPATCH_TOK = 512
COMBINE_TOK = 256
VMEM_LIMIT = 56 * 1024 * 1024


def _sigmoid(v):
    return 1.0 / (1.0 + jnp.exp(-v))


def _mixer_kernel(x_ref, pos_ref, gmix_ref, win_ref, wqkvt_ref, wdw_ref, bdw_ref, gln_ref,
                  bln_ref, wco_ref, sinks_ref, wao_ref, wout_ref, gffn_ref, wrt_ref, brt_ref,
                  h_ref, u2_ref, topi_ref, topg_ref,
                  cbuf, kprev, vprev, conv_scr, o_scr):
    s = pl.program_id(1)

    @pl.when(s == 0)
    def _():
        cbuf[0:CONV_HALO, :] = jnp.zeros((CONV_HALO, D_MODEL), F32)
        kprev[...] = jnp.zeros_like(kprev)
        vprev[...] = jnp.zeros_like(vprev)

    x = x_ref[...]
    ms = jnp.mean(x * x, axis=-1, keepdims=True)
    u = (x * lax.rsqrt(ms + RMS_EPS) * gmix_ref[...]).astype(BF16)

    def proj(lo, hi):
        return jnp.dot(u, win_ref[:, lo:hi], preferred_element_type=F32)

    n_buf = TQ + CONV_HALO
    nt = (((1,), (1,)), ((), ()))
    tn = (((0,), (0,)), ((), ()))
    qkv_rows = (COL_G - COL_Q) // CHUNKS
    gate_cols = (IN_COLS - COL_G) // CHUNKS
    qkv_parts, gate_parts = [], []
    for lt in range(CHUNKS):
        lanes = slice(lt * LANES, (lt + 1) * LANES)
        if lt % GLU_TILES == 0:
            ch = slice(lt * LANES, (lt + GLU_TILES) * LANES)
            cbuf[CONV_HALO:CONV_HALO + TQ, ch] = (
                proj(COL_VAL + ch.start, COL_VAL + ch.stop)
                * _sigmoid(proj(COL_GATE + ch.start, COL_GATE + ch.stop)))
        col = cbuf[:, lanes]
        shifted = [col] + [pltpu.roll(col, n_buf - r, 0) for r in range(1, SUBLANES)]
        acc = jnp.zeros((TQ, LANES), F32)
        for j in range(CONV_WIDTH):
            blk, r = divmod(CONV_HALO - (CONV_WIDTH - 1) + j, SUBLANES)
            acc = acc + wdw_ref[j:j + 1, lanes] * shifted[r][blk * SUBLANES:blk * SUBLANES + TQ]
        conv_scr[:, lanes] = acc
        rows = slice(lt * qkv_rows, (lt + 1) * qkv_rows)
        qkv_parts.append(lax.dot_general(wqkvt_ref[rows, :], u, nt, preferred_element_type=F32))
        gate_parts.append(proj(COL_G + lt * gate_cols, COL_G + (lt + 1) * gate_cols))
    cbuf[0:CONV_HALO, :] = cbuf[TQ:TQ + CONV_HALO, :]
    cv = conv_scr[...] + bdw_ref[...]
    mu = jnp.mean(cv, axis=-1, keepdims=True)
    cc = cv - mu
    var = jnp.mean(cc * cc, axis=-1, keepdims=True)
    cn = cc * lax.rsqrt(var + LN_EPS) * gln_ref[...] + bln_ref[...]
    cn = cn * _sigmoid(cn)
    y_conv = jnp.dot(cn.astype(BF16), wco_ref[...], preferred_element_type=F32)

    qkv_t = jnp.concatenate(qkv_parts, axis=0)
    half = ROT_DIM // 2
    pair = lax.broadcasted_iota(jnp.int32, (half, TQ), 0).astype(F32) * 2.0
    inv_freq = jnp.power(jnp.float32(ROPE_THETA), -pair / ROT_DIM)
    ang = pos_ref[...].astype(F32) * inv_freq
    cos = jnp.cos(ang)
    sin = jnp.sin(ang)

    def rotary(t):
        x1, x2 = t[0:half], t[half:ROT_DIM]
        return jnp.concatenate([x1 * cos - x2 * sin, x2 * cos + x1 * sin, t[ROT_DIM:]], axis=0)

    def head_rows(h):
        return slice(h * HEAD_DIM, (h + 1) * HEAD_DIM)

    k_row0 = N_Q_HEADS * HEAD_DIM
    v_row0 = k_row0 + N_KV_HEADS * HEAD_DIM
    q_t = jnp.concatenate([rotary(qkv_t[head_rows(h)]) for h in range(N_Q_HEADS)],
                          axis=0).astype(BF16)
    scale = HEAD_DIM ** -0.5
    k_t = jnp.concatenate(
        [rotary(qkv_t[k_row0 + g * HEAD_DIM:k_row0 + (g + 1) * HEAD_DIM]) * scale
         for g in range(N_KV_HEADS)], axis=0).astype(BF16)
    v_t = qkv_t[v_row0:v_row0 + N_KV_HEADS * HEAD_DIM].astype(BF16)

    group = N_Q_HEADS // N_KV_HEADS
    ki = lax.broadcasted_iota(jnp.int32, (2 * WINDOW, group * WINDOW), 0)
    qi = lax.broadcasted_iota(jnp.int32, (2 * WINDOW, group * WINDOW), 1) % WINDOW
    band = (ki > qi) & (ki <= qi + WINDOW)
    for qb in range(TQ // WINDOW):
        cols = slice(qb * WINDOW, (qb + 1) * WINDOW)
        if qb == 0:
            k_prev, v_prev = kprev[...], vprev[...]
            mask = band & (ki >= jnp.where(s > 0, 0, WINDOW))
        else:
            prev = slice((qb - 1) * WINDOW, qb * WINDOW)
            k_prev, v_prev = k_t[:, prev], v_t[:, prev]
            mask = band
        for g in range(N_KV_HEADS):
            k_cat = jnp.concatenate([k_prev[head_rows(g)], k_t[head_rows(g), cols]], axis=1)
            v_cat = jnp.concatenate([v_prev[head_rows(g)], v_t[head_rows(g), cols]], axis=1)
            q_blk = jnp.concatenate(
                [q_t[head_rows(g * group + i), cols] for i in range(group)], axis=1)
            sc = lax.dot_general(k_cat, q_blk, tn, preferred_element_type=F32)
            sc = jnp.where(mask, sc, -jnp.inf)
            sink = jnp.concatenate(
                [jnp.full((1, WINDOW), sinks_ref[g * group + i], F32) for i in range(group)],
                axis=1)
            m = jnp.maximum(jnp.max(sc, axis=0, keepdims=True), sink)
            e = jnp.exp(sc - m)
            den = jnp.sum(e, axis=0, keepdims=True) + jnp.exp(sink - m)
            p_t = (e * (1.0 / den)).astype(BF16)
            o_t = jnp.dot(v_cat, p_t, preferred_element_type=F32)
            for i in range(group):
                o_scr[head_rows(g * group + i), cols] = (
                    o_t[:, i * WINDOW:(i + 1) * WINDOW].astype(BF16))
    last = slice(TQ - WINDOW, TQ)
    kprev[...] = k_t[:, last]
    vprev[...] = v_t[:, last]
    y_attn = lax.dot_general(o_scr[...], wao_ref[...], tn, preferred_element_type=F32)

    gates = _sigmoid(jnp.concatenate(gate_parts, axis=1))
    mix = gates[:, :D_MODEL] * y_conv + gates[:, D_MODEL:] * y_attn
    h = x + jnp.dot(mix.astype(BF16), wout_ref[...], preferred_element_type=F32)
    h_ref[...] = h

    ms2 = jnp.mean(h * h, axis=-1, keepdims=True)
    u2 = h * lax.rsqrt(ms2 + RMS_EPS) * gffn_ref[...]
    for cidx in range(CHUNKS):
        u2_ref[pl.ds(cidx, TQ, stride=CHUNKS), :] = u2[:, cidx * LANES:(cidx + 1) * LANES]
    u2_hi = u2.astype(BF16)
    u2_lo = (u2 - u2_hi.astype(F32)).astype(BF16)
    wrt = wrt_ref[...]
    wrt_hi = wrt.astype(BF16)
    wrt_lo = (wrt - wrt_hi.astype(F32)).astype(BF16)
    logits = (lax.dot_general(wrt_hi, u2_hi, nt, preferred_element_type=F32)
              + lax.dot_general(wrt_hi, u2_lo, nt, preferred_element_type=F32)
              + lax.dot_general(wrt_lo, u2_hi, nt, preferred_element_type=F32)
              + brt_ref[...])
    eidx = lax.broadcasted_iota(jnp.int32, (N_EXPERTS, TQ), 0)
    vals, idxs = [], []
    for _ in range(TOP_K):
        m = jnp.max(logits, axis=0, keepdims=True)
        sel = jnp.min(jnp.where(logits == m, eidx, N_EXPERTS), axis=0, keepdims=True)
        vals.append(m)
        idxs.append(sel)
        logits = jnp.where(eidx == sel, -jnp.inf, logits)
    ex = [jnp.exp(v - vals[0]) for v in vals]
    den = ex[0] + ex[1] + ex[2] + ex[3]
    topi_ref[...] = jnp.concatenate(idxs, axis=0)
    topg_ref[...] = jnp.concatenate([e / den for e in ex], axis=0)


def _const_spec(shape):
    return pl.BlockSpec(shape, lambda b, s: (0,) * len(shape))


def _mixer(x, pos3, gmix, win, wqkvt, wdw, bdw, gln, bln, wco, sinks, wao, wout, gffn, wrt, brt):
    B, S, D = x.shape
    T = B * S
    n_s = S // TQ
    row_map = lambda b, s: (b * n_s + s, 0)
    return pl.pallas_call(
        _mixer_kernel,
        grid=(B, n_s),
        in_specs=[
            pl.BlockSpec((None, TQ, D), lambda b, s: (b, s, 0)),
            pl.BlockSpec((None, 1, TQ), lambda b, s: (b, 0, s)),
            _const_spec((1, D)),
            _const_spec((D, IN_COLS)),
            _const_spec((COL_G - COL_Q, D)),
            _const_spec((CONV_WIDTH, D)),
            _const_spec((1, D)),
            _const_spec((1, D)),
            _const_spec((1, D)),
            _const_spec((D, D)),
            pl.BlockSpec(memory_space=pltpu.SMEM),
            _const_spec((D, D)),
            _const_spec((D, D)),
            _const_spec((1, D)),
            _const_spec((N_EXPERTS, D)),
            _const_spec((N_EXPERTS, 1)),
        ],
        out_specs=[
            pl.BlockSpec((TQ, D), row_map),
            pl.BlockSpec((TQ * CHUNKS, LANES), row_map),
            pl.BlockSpec((TOP_K, TQ), lambda b, s: (0, b * n_s + s)),
            pl.BlockSpec((TOP_K, TQ), lambda b, s: (0, b * n_s + s)),
        ],
        out_shape=[
            jax.ShapeDtypeStruct((T, D), F32),
            jax.ShapeDtypeStruct((T * CHUNKS, LANES), F32),
            jax.ShapeDtypeStruct((TOP_K, T), jnp.int32),
            jax.ShapeDtypeStruct((TOP_K, T), F32),
        ],
        scratch_shapes=[
            pltpu.VMEM((TQ + CONV_HALO, D), F32),
            pltpu.VMEM((WINDOW, LANES), BF16),
            pltpu.VMEM((WINDOW, LANES), BF16),
            pltpu.VMEM((TQ, D), F32),
            pltpu.VMEM((D, TQ), BF16),
        ],
        compiler_params=pltpu.CompilerParams(
            dimension_semantics=("arbitrary", "arbitrary"),
            vmem_limit_bytes=VMEM_LIMIT),
        name="mixer",
    )(x, pos3, gmix, win, wqkvt, wdw, bdw, gln, bln, wco, sinks, wao, wout, gffn, wrt, brt)


def _route_kernel(topi_ref, dest_ref, blk_ref, ends_ref, cnt, base, *, n_blk_pad):
    phase = pl.program_id(0)
    step = pl.program_id(1)
    n_sub = ROUTE_TOK // ROUTE_SUB
    eidx = lax.broadcasted_iota(jnp.int32, (N_EXPERTS, ROUTE_SUB), 0)

    def onehot(j, sub):
        idx = topi_ref[j:j + 1, sub * ROUTE_SUB:(sub + 1) * ROUTE_SUB]
        return eidx == idx

    @pl.when((phase == 0) & (step == 0))
    def _():
        cnt[...] = jnp.zeros_like(cnt)

    @pl.when(phase == 0)
    def _():
        tot = jnp.zeros((N_EXPERTS, 1), F32)
        for sub in range(n_sub):
            for j in range(TOP_K):
                tot = tot + jnp.sum(onehot(j, sub).astype(F32), axis=1, keepdims=True)
        cnt[...] = cnt[...] + tot

    @pl.when((phase == 1) & (step == 0))
    def _():
        counts = cnt[...]
        padded = jnp.floor((counts + (BM - 1)) * (1.0 / BM)) * BM
        sub_i = lax.broadcasted_iota(jnp.int32, (N_EXPERTS, LANES), 0)
        lane_i = lax.broadcasted_iota(jnp.int32, (N_EXPERTS, LANES), 1)
        row = jnp.sum(jnp.where(sub_i == lane_i, padded, 0.0), axis=0, keepdims=True)
        ends = jnp.sum(jnp.where(lane_i <= sub_i, row, 0.0), axis=1, keepdims=True)
        base[...] = jnp.broadcast_to(ends, (N_EXPERTS, LANES)) - padded
        ends_row = jnp.sum(jnp.where(sub_i == lane_i, ends, 0.0), axis=0, keepdims=True)
        ends_ref[...] = ends_row.astype(jnp.int32)
        blk_start = (lax.broadcasted_iota(jnp.int32, (N_EXPERTS, n_blk_pad), 1) * BM).astype(F32)
        blk = jnp.sum((blk_start >= ends).astype(jnp.int32), axis=0, keepdims=True)
        blk_ref[...] = jnp.minimum(blk, N_EXPERTS - 1)

    @pl.when(phase == 1)
    def _():
        r = lax.broadcasted_iota(jnp.int32, (ROUTE_SUB, ROUTE_SUB), 0)
        cidx = lax.broadcasted_iota(jnp.int32, (ROUTE_SUB, ROUTE_SUB), 1)
        upper = jnp.where(r < cidx, 1.0, 0.0).astype(BF16)
        run = base[...][:, 0:1]
        for sub in range(n_sub):
            for j in range(TOP_K):
                oh = onehot(j, sub)
                ohf = oh.astype(F32)
                pre = jnp.dot(ohf.astype(BF16), upper, preferred_element_type=F32)
                rank = jnp.sum(jnp.where(oh, pre + run, 0.0), axis=0, keepdims=True)
                dest_ref[j:j + 1, sub * ROUTE_SUB:(sub + 1) * ROUTE_SUB] = rank.astype(jnp.int32)
                run = run + jnp.sum(ohf, axis=1, keepdims=True)
        base[...] = jnp.broadcast_to(run, (N_EXPERTS, LANES))


def _route(topi, n_blk_pad):
    T = topi.shape[1]
    n_steps = T // ROUTE_TOK
    return pl.pallas_call(
        functools.partial(_route_kernel, n_blk_pad=n_blk_pad),
        grid=(2, n_steps),
        in_specs=[pl.BlockSpec((TOP_K, ROUTE_TOK), lambda p, c: (0, c))],
        out_specs=[
            pl.BlockSpec((TOP_K, ROUTE_TOK), lambda p, c: (0, c * p)),
            pl.BlockSpec((1, n_blk_pad), lambda p, c: (0, 0)),
            pl.BlockSpec((1, LANES), lambda p, c: (0, 0)),
        ],
        out_shape=[
            jax.ShapeDtypeStruct((TOP_K, T), jnp.int32),
            jax.ShapeDtypeStruct((1, n_blk_pad), jnp.int32),
            jax.ShapeDtypeStruct((1, LANES), jnp.int32),
        ],
        scratch_shapes=[
            pltpu.VMEM((N_EXPERTS, LANES), F32),
            pltpu.VMEM((N_EXPERTS, LANES), F32),
        ],
        compiler_params=pltpu.CompilerParams(
            dimension_semantics=("arbitrary", "arbitrary")),
        name="route",
    )(topi)


def _slab(ref, row):
    return ref.at[pl.ds(pl.multiple_of(row * CHUNKS, CHUNKS), CHUNKS), :]


def _dispatch_kernel(dest_ref, ends_ref, u2_ref, xs_hbm, zbuf, zsem, sem):
    step = pl.program_id(0)

    @pl.when(step == 0)
    def _():
        zbuf[...] = jnp.zeros_like(zbuf)

        def zero_copy(e):
            start = pl.multiple_of((ends_ref[0, e] - BM) * CHUNKS, BM * CHUNKS)
            return pltpu.make_async_copy(zbuf, xs_hbm.at[pl.ds(start, BM * CHUNKS), :], zsem)

        def nonempty(e):
            prev_end = jnp.where(e > 0, ends_ref[0, jnp.maximum(e - 1, 0)], 0)
            return ends_ref[0, e] > prev_end

        def start_body(e, carry):
            @pl.when(nonempty(e))
            def _():
                zero_copy(e).start()
            return carry

        def wait_body(e, carry):
            @pl.when(nonempty(e))
            def _():
                zero_copy(e).wait()
            return carry

        lax.fori_loop(0, N_EXPERTS, start_body, 0)
        lax.fori_loop(0, N_EXPERTS, wait_body, 0)

        def tail_copy(b):
            start = pl.multiple_of(b * (BM * CHUNKS), BM * CHUNKS)
            return pltpu.make_async_copy(zbuf, xs_hbm.at[pl.ds(start, BM * CHUNKS), :], zsem)

        def tail_start(b, carry):
            tail_copy(b).start()
            return carry

        def tail_wait(b, carry):
            tail_copy(b).wait()
            return carry

        n_used = ends_ref[0, N_EXPERTS - 1] // BM
        n_blk = xs_hbm.shape[0] // (BM * CHUNKS)
        lax.fori_loop(n_used, n_blk, tail_start, 0)
        lax.fori_loop(n_used, n_blk, tail_wait, 0)

    def row_copy(t, j):
        return pltpu.make_async_copy(_slab(u2_ref, t), _slab(xs_hbm, dest_ref[j, t]), sem)

    def start_body(t, carry):
        for j in range(TOP_K):
            row_copy(t, j).start(priority=j % 2)
        return carry

    def wait_body(t, carry):
        for j in range(TOP_K):
            row_copy(t, j).wait()
        return carry

    lax.fori_loop(0, DISPATCH_TOK, start_body, 0, unroll=2)
    lax.fori_loop(0, DISPATCH_TOK, wait_body, 0)


def _dispatch(dest, ends, u2s, n_rows):
    T = dest.shape[1]
    return pl.pallas_call(
        _dispatch_kernel,
        grid=(T // DISPATCH_TOK,),
        in_specs=[
            pl.BlockSpec((TOP_K, DISPATCH_TOK), lambda i: (0, i), memory_space=pltpu.SMEM),
            pl.BlockSpec(memory_space=pltpu.SMEM),
            pl.BlockSpec((DISPATCH_TOK * CHUNKS, LANES), lambda i: (i, 0)),
        ],
        out_specs=pl.BlockSpec(memory_space=pl.ANY),
        out_shape=jax.ShapeDtypeStruct((n_rows * CHUNKS, LANES), F32),
        scratch_shapes=[
            pltpu.VMEM((BM * CHUNKS, LANES), F32),
            pltpu.SemaphoreType.DMA,
            pltpu.SemaphoreType.DMA,
        ],
        compiler_params=pltpu.CompilerParams(dimension_semantics=("arbitrary",)),
        name="dispatch",
    )(dest, ends, u2s)


def _expert_kernel(blk_ref, ends_ref, xs_ref, w1_hbm, b1_ref, w2_hbm, b2_ref, ys_ref,
                   w1_f32, w2_f32, w1_bf, w2_bf, slot_ref, sems):
    i = pl.program_id(0)
    n_used = ends_ref[0, N_EXPERTS - 1] // BM
    expert = blk_ref[0, i]
    new_expert = (i == 0) | (expert != blk_ref[0, jnp.maximum(i - 1, 0)])

    def weight_copies(e, slot):
        return (pltpu.make_async_copy(w1_hbm.at[e], w1_f32.at[slot], sems.at[slot]),
                pltpu.make_async_copy(w2_hbm.at[e], w2_f32.at[slot], sems.at[slot]))

    @pl.when(i == 0)
    def _():
        slot_ref[0] = 0
        for cp in weight_copies(expert, 0):
            cp.start()

    @pl.when((i < n_used) & new_expert)
    def _():
        slot = slot_ref[0]
        for cp in weight_copies(expert, slot):
            cp.wait()
        i_next = ends_ref[0, expert] // BM

        @pl.when(i_next < n_used)
        def _():
            for cp in weight_copies(blk_ref[0, i_next], 1 - slot):
                cp.start()

        w1_bf[...] = w1_f32[slot].astype(BF16)
        w2_bf[...] = w2_f32[slot].astype(BF16)
        slot_ref[0] = 1 - slot

    @pl.when(i < n_used)
    def _():
        xb = jnp.concatenate(
            [xs_ref[pl.ds(cidx, BM, stride=CHUNKS), :] for cidx in range(CHUNKS)],
            axis=1).astype(BF16)
        h1 = jnp.dot(xb, w1_bf[...], preferred_element_type=F32) + b1_ref[...]
        f = h1.shape[1] // 2
        glu = jnp.minimum(h1[:, :f], SWIGLU_LIMIT)
        lin = jnp.clip(h1[:, f:], -SWIGLU_LIMIT, SWIGLU_LIMIT)
        act = glu * _sigmoid(SWIGLU_ALPHA * glu) * (lin + 1.0)
        y = jnp.dot(act.astype(BF16), w2_bf[...], preferred_element_type=F32) + b2_ref[...]
        for cidx in range(CHUNKS):
            ys_ref[pl.ds(cidx, BM, stride=CHUNKS), :] = y[:, cidx * LANES:(cidx + 1) * LANES]

    @pl.when(i >= n_used)
    def _():
        ys_ref[...] = jnp.zeros_like(ys_ref)


def _experts(blk_e, ends, xs, w1, b1, w2, b2, n_blk):
    E, D, F2 = w1.shape

    def used(i, ends_ref):
        return jnp.minimum(i, ends_ref[0, N_EXPERTS - 1] // BM - 1)

    grid_spec = pltpu.PrefetchScalarGridSpec(
        num_scalar_prefetch=2,
        grid=(n_blk,),
        in_specs=[
            pl.BlockSpec((BM * CHUNKS, LANES), lambda i, blk, ends: (used(i, ends), 0)),
            pl.BlockSpec(memory_space=pl.ANY),
            pl.BlockSpec((None, 1, F2), lambda i, blk, ends: (blk[0, used(i, ends)], 0, 0)),
            pl.BlockSpec(memory_space=pl.ANY),
            pl.BlockSpec((None, 1, D), lambda i, blk, ends: (blk[0, used(i, ends)], 0, 0)),
        ],
        out_specs=pl.BlockSpec((BM * CHUNKS, LANES), lambda i, blk, ends: (i, 0)),
        scratch_shapes=[
            pltpu.VMEM((2, D, F2), F32),
            pltpu.VMEM((2, F2 // 2, D), F32),
            pltpu.VMEM((D, F2), BF16),
            pltpu.VMEM((F2 // 2, D), BF16),
            pltpu.SMEM((1,), jnp.int32),
            pltpu.SemaphoreType.DMA((2,)),
        ],
    )
    return pl.pallas_call(
        _expert_kernel,
        grid_spec=grid_spec,
        out_shape=jax.ShapeDtypeStruct((n_blk * BM * CHUNKS, LANES), F32),
        compiler_params=pltpu.CompilerParams(
            dimension_semantics=("arbitrary",), vmem_limit_bytes=VMEM_LIMIT),
        name="experts",
    )(blk_e, ends, xs, w1, b1, w2, b2)


def _combine_kernel(dest_ref, dest_next_ref, gate_ref, h_ref, gfin_ref, ys_hbm, out_ref,
                    gbuf0, gbuf1, sem0, sem1):
    k = pl.program_id(0)
    tok = COMBINE_TOK

    def issue(d_ref, off, gbuf, sem):
        def body(t, carry):
            for j in range(TOP_K):
                pltpu.make_async_copy(_slab(ys_hbm, d_ref[j, off + t]), _slab(gbuf, j * tok + t),
                                      sem).start(priority=j % 2)
            return carry
        lax.fori_loop(0, tok, body, 0, unroll=2)

    def drain(gbuf, sem):
        def body(t, carry):
            for j in range(TOP_K):
                pltpu.make_async_copy(_slab(ys_hbm, 0), _slab(gbuf, j * tok + t), sem).wait()
            return carry
        lax.fori_loop(0, tok, body, 0)

    def finish(gbuf, off):
        rows = slice(off, off + tok)
        gate = gate_ref[rows, :]
        parts = []
        ss = jnp.zeros((tok, 1), F32)
        for cidx in range(CHUNKS):
            acc = h_ref[rows, cidx * LANES:(cidx + 1) * LANES]
            for j in range(TOP_K):
                acc = acc + gate[:, j:j + 1] * gbuf[pl.ds(j * tok * CHUNKS + cidx, tok,
                                                          stride=CHUNKS), :]
            parts.append(acc)
            ss = ss + jnp.sum(acc * acc, axis=-1, keepdims=True)
        scale = lax.rsqrt(ss * (1.0 / D_MODEL) + RMS_EPS)
        for cidx in range(CHUNKS):
            lanes = slice(cidx * LANES, (cidx + 1) * LANES)
            out_ref[rows, lanes] = parts[cidx] * scale * gfin_ref[:, lanes]

    @pl.when(k == 0)
    def _():
        issue(dest_ref, 0, gbuf0, sem0)

    issue(dest_ref, tok, gbuf1, sem1)
    drain(gbuf0, sem0)
    finish(gbuf0, 0)

    @pl.when(k + 1 < pl.num_programs(0))
    def _():
        issue(dest_next_ref, 0, gbuf0, sem0)

    drain(gbuf1, sem1)
    finish(gbuf1, tok)


def _combine(dest, gate_t, h, gfin, ys):
    T, D = h.shape
    step_tok = 2 * COMBINE_TOK
    n_steps = T // step_tok
    gbuf = pltpu.VMEM((TOP_K * COMBINE_TOK * CHUNKS, LANES), F32)
    return pl.pallas_call(
        _combine_kernel,
        grid=(n_steps,),
        in_specs=[
            pl.BlockSpec((TOP_K, step_tok), lambda i: (0, i), memory_space=pltpu.SMEM),
            pl.BlockSpec((TOP_K, step_tok), lambda i: (0, jnp.minimum(i + 1, n_steps - 1)),
                         memory_space=pltpu.SMEM),
            pl.BlockSpec((step_tok, TOP_K), lambda i: (i, 0)),
            pl.BlockSpec((step_tok, D), lambda i: (i, 0)),
            pl.BlockSpec((1, D), lambda i: (0, 0)),
            pl.BlockSpec(memory_space=pl.ANY),
        ],
        out_specs=pl.BlockSpec((step_tok, D), lambda i: (i, 0)),
        out_shape=jax.ShapeDtypeStruct((T, D), F32),
        scratch_shapes=[gbuf, gbuf, pltpu.SemaphoreType.DMA, pltpu.SemaphoreType.DMA],
        compiler_params=pltpu.CompilerParams(dimension_semantics=("arbitrary",)),
        name="combine",
    )(dest, dest, gate_t, h, gfin, ys)


def kernel(x, positions, g_mix_norm, w_in, w_dw, b_dw, g_conv_ln, b_conv_ln, w_conv_out,
           attn_sinks, w_attn_out, w_out, g_ffn_norm, w_router, b_router,
           w_mlp1, b_mlp1, w_mlp2, b_mlp2, g_final):
    B, S, D = x.shape
    T = B * S
    depth = g_mix_norm.shape[0]
    assert depth == 1 and D == D_MODEL and S % TQ == 0 and T % ROUTE_TOK == 0
    assert T % DISPATCH_TOK == 0 and T % (2 * COMBINE_TOK) == 0
    l = 0
    n_blk = (T * TOP_K) // BM + N_EXPERTS
    n_blk_pad = -(-n_blk // LANES) * LANES

    h, u2s, topi, topg = _mixer(
        x, positions[:, None, :], g_mix_norm[l][None, :],
        w_in[l].astype(BF16), w_in[l][:, COL_Q:COL_G].T.astype(BF16), w_dw[l],
        b_dw[l][None, :], g_conv_ln[l][None, :], b_conv_ln[l][None, :],
        w_conv_out[l].astype(BF16), attn_sinks[l],
        w_attn_out[l].astype(BF16), w_out[l].astype(BF16), g_ffn_norm[l][None, :],
        w_router[l].T, b_router[l][:, None])
    dest, blk_e, ends = _route(topi, n_blk_pad)
    xs = _dispatch(dest, ends, u2s, n_blk * BM)
    ys = _experts(blk_e, ends, xs, w_mlp1[l], b_mlp1[l][:, None, :],
                  w_mlp2[l], b_mlp2[l][:, None, :], n_blk)
    out = _combine(dest, topg.T, h, g_final[None, :], ys)
    return out.reshape(B, S, D)
```

```python
import functools

import jax
import jax.numpy as jnp
import numpy as np
from jax import lax
from jax.experimental import pallas as pl
from jax.experimental.pallas import tpu as pltpu

F32 = jnp.float32
BF16 = jnp.bfloat16

D_MODEL = 1024
CONV_WIDTH = 31
HEAD_DIM = 64
N_Q_HEADS = 16
N_KV_HEADS = 2
WINDOW = 128
ROT_DIM = 16
ROPE_THETA = 500000.0
N_EXPERTS = 32
TOP_K = 4
SWIGLU_LIMIT = 7.0
SWIGLU_ALPHA = 1.702
RMS_EPS = 1e-5
LN_EPS = 1e-5

LANES = 128
SUBLANES = 8
CHUNKS = D_MODEL // LANES

COL_VAL = 0
COL_GATE = D_MODEL
COL_Q = 2 * D_MODEL
COL_K = COL_Q + N_Q_HEADS * HEAD_DIM
COL_V = COL_K + N_KV_HEADS * HEAD_DIM
COL_G = COL_V + N_KV_HEADS * HEAD_DIM
IN_COLS = COL_G + 2 * D_MODEL

TQ = 256
CONV_HALO = 32
GLU_TILES = 2
ROUTE_TOK = 2048
ROUTE_SUB = 512
BM = 256
DISPATCH_TOK = 512
COMBINE_TOK = 256
VMEM_LIMIT = 56 * 1024 * 1024


def _sigmoid(v):
    return 1.0 / (1.0 + jnp.exp(-v))


def _mixer_kernel(x_ref, pos_ref, gmix_ref, win_ref, wqkvt_ref, wdw_ref, bdw_ref, gln_ref,
                  bln_ref, wco_ref, sinks_ref, wao_ref, wout_ref, gffn_ref, wrt_ref, brt_ref,
                  h_ref, u2_ref, topi_ref, topg_ref,
                  cbuf, kprev, vprev, conv_scr, o_scr):
    s = pl.program_id(1)

    @pl.when(s == 0)
    def _():
        cbuf[0:CONV_HALO, :] = jnp.zeros((CONV_HALO, D_MODEL), F32)
        kprev[...] = jnp.zeros_like(kprev)
        vprev[...] = jnp.zeros_like(vprev)

    x = x_ref[...]
    ms = jnp.mean(x * x, axis=-1, keepdims=True)
    u = (x * lax.rsqrt(ms + RMS_EPS) * gmix_ref[...]).astype(BF16)

    def proj(lo, hi):
        return jnp.dot(u, win_ref[:, lo:hi], preferred_element_type=F32)

    n_buf = TQ + CONV_HALO
    nt = (((1,), (1,)), ((), ()))
    tn = (((0,), (0,)), ((), ()))
    qkv_rows = (COL_G - COL_Q) // CHUNKS
    gate_cols = (IN_COLS - COL_G) // CHUNKS
    qkv_parts, gate_parts = [], []
    for lt in range(CHUNKS):
        lanes = slice(lt * LANES, (lt + 1) * LANES)
        if lt % GLU_TILES == 0:
            ch = slice(lt * LANES, (lt + GLU_TILES) * LANES)
            cbuf[CONV_HALO:CONV_HALO + TQ, ch] = (
                proj(COL_VAL + ch.start, COL_VAL + ch.stop)
                * _sigmoid(proj(COL_GATE + ch.start, COL_GATE + ch.stop)))
        col = cbuf[:, lanes]
        shifted = [col] + [pltpu.roll(col, n_buf - r, 0) for r in range(1, SUBLANES)]
        acc = jnp.zeros((TQ, LANES), F32)
        for j in range(CONV_WIDTH):
            blk, r = divmod(CONV_HALO - (CONV_WIDTH - 1) + j, SUBLANES)
            acc = acc + wdw_ref[j:j + 1, lanes] * shifted[r][blk * SUBLANES:blk * SUBLANES + TQ]
        conv_scr[:, lanes] = acc
        rows = slice(lt * qkv_rows, (lt + 1) * qkv_rows)
        qkv_parts.append(lax.dot_general(wqkvt_ref[rows, :], u, nt, preferred_element_type=F32))
        gate_parts.append(proj(COL_G + lt * gate_cols, COL_G + (lt + 1) * gate_cols))
    cbuf[0:CONV_HALO, :] = cbuf[TQ:TQ + CONV_HALO, :]
    cv = conv_scr[...] + bdw_ref[...]
    mu = jnp.mean(cv, axis=-1, keepdims=True)
    cc = cv - mu
    var = jnp.mean(cc * cc, axis=-1, keepdims=True)
    cn = cc * lax.rsqrt(var + LN_EPS) * gln_ref[...] + bln_ref[...]
    cn = cn * _sigmoid(cn)
    y_conv = jnp.dot(cn.astype(BF16), wco_ref[...], preferred_element_type=F32)

    qkv_t = jnp.concatenate(qkv_parts, axis=0)
    half = ROT_DIM // 2
    pair = lax.broadcasted_iota(jnp.int32, (half, TQ), 0).astype(F32) * 2.0
    inv_freq = jnp.power(jnp.float32(ROPE_THETA), -pair / ROT_DIM)
    ang = pos_ref[...].astype(F32) * inv_freq
    cos = jnp.cos(ang)
    sin = jnp.sin(ang)

    def rotary(t):
        x1, x2 = t[0:half], t[half:ROT_DIM]
        return jnp.concatenate([x1 * cos - x2 * sin, x2 * cos + x1 * sin, t[ROT_DIM:]], axis=0)

    def head_rows(h):
        return slice(h * HEAD_DIM, (h + 1) * HEAD_DIM)

    k_row0 = N_Q_HEADS * HEAD_DIM
    v_row0 = k_row0 + N_KV_HEADS * HEAD_DIM
    q_t = jnp.concatenate([rotary(qkv_t[head_rows(h)]) for h in range(N_Q_HEADS)],
                          axis=0).astype(BF16)
    scale = HEAD_DIM ** -0.5
    k_t = jnp.concatenate(
        [rotary(qkv_t[k_row0 + g * HEAD_DIM:k_row0 + (g + 1) * HEAD_DIM]) * scale
         for g in range(N_KV_HEADS)], axis=0).astype(BF16)
    v_t = qkv_t[v_row0:v_row0 + N_KV_HEADS * HEAD_DIM].astype(BF16)

    group = N_Q_HEADS // N_KV_HEADS
    ki = lax.broadcasted_iota(jnp.int32, (2 * WINDOW, group * WINDOW), 0)
    qi = lax.broadcasted_iota(jnp.int32, (2 * WINDOW, group * WINDOW), 1) % WINDOW
    band = (ki > qi) & (ki <= qi + WINDOW)
    for qb in range(TQ // WINDOW):
        cols = slice(qb * WINDOW, (qb + 1) * WINDOW)
        if qb == 0:
            k_prev, v_prev = kprev[...], vprev[...]
            mask = band & (ki >= jnp.where(s > 0, 0, WINDOW))
        else:
            prev = slice((qb - 1) * WINDOW, qb * WINDOW)
            k_prev, v_prev = k_t[:, prev], v_t[:, prev]
            mask = band
        for g in range(N_KV_HEADS):
            k_cat = jnp.concatenate([k_prev[head_rows(g)], k_t[head_rows(g), cols]], axis=1)
            v_cat = jnp.concatenate([v_prev[head_rows(g)], v_t[head_rows(g), cols]], axis=1)
            q_blk = jnp.concatenate(
                [q_t[head_rows(g * group + i), cols] for i in range(group)], axis=1)
            sc = lax.dot_general(k_cat, q_blk, tn, preferred_element_type=F32)
            sc = jnp.where(mask, sc, -jnp.inf)
            sink = jnp.concatenate(
                [jnp.full((1, WINDOW), sinks_ref[g * group + i], F32) for i in range(group)],
                axis=1)
            m = jnp.maximum(jnp.max(sc, axis=0, keepdims=True), sink)
            e = jnp.exp(sc - m)
            den = jnp.sum(e, axis=0, keepdims=True) + jnp.exp(sink - m)
            p_t = (e * (1.0 / den)).astype(BF16)
            o_t = jnp.dot(v_cat, p_t, preferred_element_type=F32)
            for i in range(group):
                o_scr[head_rows(g * group + i), cols] = (
                    o_t[:, i * WINDOW:(i + 1) * WINDOW].astype(BF16))
    last = slice(TQ - WINDOW, TQ)
    kprev[...] = k_t[:, last]
    vprev[...] = v_t[:, last]
    y_attn = lax.dot_general(o_scr[...], wao_ref[...], tn, preferred_element_type=F32)

    gates = _sigmoid(jnp.concatenate(gate_parts, axis=1))
    mix = gates[:, :D_MODEL] * y_conv + gates[:, D_MODEL:] * y_attn
    h = x + jnp.dot(mix.astype(BF16), wout_ref[...], preferred_element_type=F32)
    h_ref[...] = h

    ms2 = jnp.mean(h * h, axis=-1, keepdims=True)
    u2 = h * lax.rsqrt(ms2 + RMS_EPS) * gffn_ref[...]
    for cidx in range(CHUNKS):
        u2_ref[pl.ds(cidx, TQ, stride=CHUNKS), :] = u2[:, cidx * LANES:(cidx + 1) * LANES]
    u2_hi = u2.astype(BF16)
    u2_lo = (u2 - u2_hi.astype(F32)).astype(BF16)
    wrt = wrt_ref[...]
    wrt_hi = wrt.astype(BF16)
    wrt_lo = (wrt - wrt_hi.astype(F32)).astype(BF16)
    logits = (lax.dot_general(wrt_hi, u2_hi, nt, preferred_element_type=F32)
              + lax.dot_general(wrt_hi, u2_lo, nt, preferred_element_type=F32)
              + lax.dot_general(wrt_lo, u2_hi, nt, preferred_element_type=F32)
              + brt_ref[...])
    eidx = lax.broadcasted_iota(jnp.int32, (N_EXPERTS, TQ), 0)
    vals, idxs = [], []
    for _ in range(TOP_K):
        m = jnp.max(logits, axis=0, keepdims=True)
        sel = jnp.min(jnp.where(logits == m, eidx, N_EXPERTS), axis=0, keepdims=True)
        vals.append(m)
        idxs.append(sel)
        logits = jnp.where(eidx == sel, -jnp.inf, logits)
    ex = [jnp.exp(v - vals[0]) for v in vals]
    den = ex[0] + ex[1] + ex[2] + ex[3]
    topi_ref[...] = jnp.concatenate(idxs, axis=0)
    topg_ref[...] = jnp.concatenate([e / den for e in ex], axis=0)


def _const_spec(shape):
    return pl.BlockSpec(shape, lambda b, s: (0,) * len(shape))


def _mixer(x, pos3, gmix, win, wqkvt, wdw, bdw, gln, bln, wco, sinks, wao, wout, gffn, wrt, brt):
    B, S, D = x.shape
    T = B * S
    n_s = S // TQ
    row_map = lambda b, s: (b * n_s + s, 0)
    return pl.pallas_call(
        _mixer_kernel,
        grid=(B, n_s),
        in_specs=[
            pl.BlockSpec((None, TQ, D), lambda b, s: (b, s, 0)),
            pl.BlockSpec((None, 1, TQ), lambda b, s: (b, 0, s)),
            _const_spec((1, D)),
            _const_spec((D, IN_COLS)),
            _const_spec((COL_G - COL_Q, D)),
            _const_spec((CONV_WIDTH, D)),
            _const_spec((1, D)),
            _const_spec((1, D)),
            _const_spec((1, D)),
            _const_spec((D, D)),
            pl.BlockSpec(memory_space=pltpu.SMEM),
            _const_spec((D, D)),
            _const_spec((D, D)),
            _const_spec((1, D)),
            _const_spec((N_EXPERTS, D)),
            _const_spec((N_EXPERTS, 1)),
        ],
        out_specs=[
            pl.BlockSpec((TQ, D), row_map),
            pl.BlockSpec((TQ * CHUNKS, LANES), row_map),
            pl.BlockSpec((TOP_K, TQ), lambda b, s: (0, b * n_s + s)),
            pl.BlockSpec((TOP_K, TQ), lambda b, s: (0, b * n_s + s)),
        ],
        out_shape=[
            jax.ShapeDtypeStruct((T, D), F32),
            jax.ShapeDtypeStruct((T * CHUNKS, LANES), F32),
            jax.ShapeDtypeStruct((TOP_K, T), jnp.int32),
            jax.ShapeDtypeStruct((TOP_K, T), F32),
        ],
        scratch_shapes=[
            pltpu.VMEM((TQ + CONV_HALO, D), F32),
            pltpu.VMEM((WINDOW, LANES), BF16),
            pltpu.VMEM((WINDOW, LANES), BF16),
            pltpu.VMEM((TQ, D), F32),
            pltpu.VMEM((D, TQ), BF16),
        ],
        compiler_params=pltpu.CompilerParams(
            dimension_semantics=("arbitrary", "arbitrary"),
            vmem_limit_bytes=VMEM_LIMIT),
        name="mixer",
    )(x, pos3, gmix, win, wqkvt, wdw, bdw, gln, bln, wco, sinks, wao, wout, gffn, wrt, brt)


def _route_kernel(topi_ref, dest_ref, blk_ref, ends_ref, cnt, base, *, n_blk_pad):
    phase = pl.program_id(0)
    step = pl.program_id(1)
    n_sub = ROUTE_TOK // ROUTE_SUB
    eidx = lax.broadcasted_iota(jnp.int32, (N_EXPERTS, ROUTE_SUB), 0)

    def onehot(j, sub):
        idx = topi_ref[j:j + 1, sub * ROUTE_SUB:(sub + 1) * ROUTE_SUB]
        return eidx == idx

    @pl.when((phase == 0) & (step == 0))
    def _():
        cnt[...] = jnp.zeros_like(cnt)

    @pl.when(phase == 0)
    def _():
        tot = jnp.zeros((N_EXPERTS, 1), F32)
        for sub in range(n_sub):
            for j in range(TOP_K):
                tot = tot + jnp.sum(onehot(j, sub).astype(F32), axis=1, keepdims=True)
        cnt[...] = cnt[...] + tot

    @pl.when((phase == 1) & (step == 0))
    def _():
        counts = cnt[...]
        padded = jnp.floor((counts + (BM - 1)) * (1.0 / BM)) * BM
        sub_i = lax.broadcasted_iota(jnp.int32, (N_EXPERTS, LANES), 0)
        lane_i = lax.broadcasted_iota(jnp.int32, (N_EXPERTS, LANES), 1)
        row = jnp.sum(jnp.where(sub_i == lane_i, padded, 0.0), axis=0, keepdims=True)
        ends = jnp.sum(jnp.where(lane_i <= sub_i, row, 0.0), axis=1, keepdims=True)
        base[...] = jnp.broadcast_to(ends, (N_EXPERTS, LANES)) - padded
        ends_row = jnp.sum(jnp.where(sub_i == lane_i, ends, 0.0), axis=0, keepdims=True)
        ends_ref[...] = ends_row.astype(jnp.int32)
        blk_start = (lax.broadcasted_iota(jnp.int32, (N_EXPERTS, n_blk_pad), 1) * BM).astype(F32)
        blk = jnp.sum((blk_start >= ends).astype(jnp.int32), axis=0, keepdims=True)
        blk_ref[...] = jnp.minimum(blk, N_EXPERTS - 1)

    @pl.when(phase == 1)
    def _():
        r = lax.broadcasted_iota(jnp.int32, (ROUTE_SUB, ROUTE_SUB), 0)
        cidx = lax.broadcasted_iota(jnp.int32, (ROUTE_SUB, ROUTE_SUB), 1)
        upper = jnp.where(r < cidx, 1.0, 0.0).astype(BF16)
        run = base[...][:, 0:1]
        for sub in range(n_sub):
            for j in range(TOP_K):
                oh = onehot(j, sub)
                ohf = oh.astype(F32)
                pre = jnp.dot(ohf.astype(BF16), upper, preferred_element_type=F32)
                rank = jnp.sum(jnp.where(oh, pre + run, 0.0), axis=0, keepdims=True)
                dest_ref[j:j + 1, sub * ROUTE_SUB:(sub + 1) * ROUTE_SUB] = rank.astype(jnp.int32)
                run = run + jnp.sum(ohf, axis=1, keepdims=True)
        base[...] = jnp.broadcast_to(run, (N_EXPERTS, LANES))


def _route(topi, n_blk_pad):
    T = topi.shape[1]
    n_steps = T // ROUTE_TOK
    return pl.pallas_call(
        functools.partial(_route_kernel, n_blk_pad=n_blk_pad),
        grid=(2, n_steps),
        in_specs=[pl.BlockSpec((TOP_K, ROUTE_TOK), lambda p, c: (0, c))],
        out_specs=[
            pl.BlockSpec((TOP_K, ROUTE_TOK), lambda p, c: (0, c * p)),
            pl.BlockSpec((1, n_blk_pad), lambda p, c: (0, 0)),
            pl.BlockSpec((1, LANES), lambda p, c: (0, 0)),
        ],
        out_shape=[
            jax.ShapeDtypeStruct((TOP_K, T), jnp.int32),
            jax.ShapeDtypeStruct((1, n_blk_pad), jnp.int32),
            jax.ShapeDtypeStruct((1, LANES), jnp.int32),
        ],
        scratch_shapes=[
            pltpu.VMEM((N_EXPERTS, LANES), F32),
            pltpu.VMEM((N_EXPERTS, LANES), F32),
        ],
        compiler_params=pltpu.CompilerParams(
            dimension_semantics=("arbitrary", "arbitrary")),
        name="route",
    )(topi)


def _slab(ref, row):
    return ref.at[pl.ds(pl.multiple_of(row * CHUNKS, CHUNKS), CHUNKS), :]


def _dispatch_kernel(dest_ref, ends_ref, u2_ref, xs_hbm, zbuf, zsem, sem):
    step = pl.program_id(0)

    @pl.when(step == 0)
    def _():
        zbuf[...] = jnp.zeros_like(zbuf)

        def zero_copy(e):
            start = pl.multiple_of((ends_ref[0, e] - BM) * CHUNKS, BM * CHUNKS)
            return pltpu.make_async_copy(zbuf, xs_hbm.at[pl.ds(start, BM * CHUNKS), :], zsem)

        def nonempty(e):
            prev_end = jnp.where(e > 0, ends_ref[0, jnp.maximum(e - 1, 0)], 0)
            return ends_ref[0, e] > prev_end

        def start_body(e, carry):
            @pl.when(nonempty(e))
            def _():
                zero_copy(e).start()
            return carry

        def wait_body(e, carry):
            @pl.when(nonempty(e))
            def _():
                zero_copy(e).wait()
            return carry

        lax.fori_loop(0, N_EXPERTS, start_body, 0)
        lax.fori_loop(0, N_EXPERTS, wait_body, 0)

        def tail_copy(b):
            start = pl.multiple_of(b * (BM * CHUNKS), BM * CHUNKS)
            return pltpu.make_async_copy(zbuf, xs_hbm.at[pl.ds(start, BM * CHUNKS), :], zsem)

        def tail_start(b, carry):
            tail_copy(b).start()
            return carry

        def tail_wait(b, carry):
            tail_copy(b).wait()
            return carry

        n_used = ends_ref[0, N_EXPERTS - 1] // BM
        n_blk = xs_hbm.shape[0] // (BM * CHUNKS)
        lax.fori_loop(n_used, n_blk, tail_start, 0)
        lax.fori_loop(n_used, n_blk, tail_wait, 0)

    def row_copy(t, j):
        return pltpu.make_async_copy(_slab(u2_ref, t), _slab(xs_hbm, dest_ref[j, t]), sem)

    def start_body(t, carry):
        for j in range(TOP_K):
            row_copy(t, j).start(priority=j % 2)
        return carry

    def wait_body(t, carry):
        for j in range(TOP_K):
            row_copy(t, j).wait()
        return carry

    lax.fori_loop(0, DISPATCH_TOK, start_body, 0, unroll=2)
    for _ in range(TOP_K):
        pltpu.make_async_copy(u2_ref, xs_hbm.at[pl.ds(0, DISPATCH_TOK * CHUNKS), :], sem).wait()


def _dispatch(dest, ends, u2s, n_rows):
    T = dest.shape[1]
    return pl.pallas_call(
        _dispatch_kernel,
        grid=(T // DISPATCH_TOK,),
        in_specs=[
            pl.BlockSpec((TOP_K, DISPATCH_TOK), lambda i: (0, i), memory_space=pltpu.SMEM),
            pl.BlockSpec(memory_space=pltpu.SMEM),
            pl.BlockSpec((DISPATCH_TOK * CHUNKS, LANES), lambda i: (i, 0)),
        ],
        out_specs=pl.BlockSpec(memory_space=pl.ANY),
        out_shape=jax.ShapeDtypeStruct((n_rows * CHUNKS, LANES), F32),
        scratch_shapes=[
            pltpu.VMEM((BM * CHUNKS, LANES), F32),
            pltpu.SemaphoreType.DMA,
            pltpu.SemaphoreType.DMA,
        ],
        compiler_params=pltpu.CompilerParams(dimension_semantics=("arbitrary",)),
        name="dispatch",
    )(dest, ends, u2s)


def _expert_kernel(blk_ref, ends_ref, xs_ref, w1_hbm, b1_ref, w2_hbm, b2_ref, ys_ref,
                   w1_f32, w2_f32, w1_bf, w2_bf, slot_ref, sems):
    i = pl.program_id(0)
    n_used = ends_ref[0, N_EXPERTS - 1] // BM
    expert = blk_ref[0, i]
    new_expert = (i == 0) | (expert != blk_ref[0, jnp.maximum(i - 1, 0)])

    def weight_copies(e, slot):
        return (pltpu.make_async_copy(w1_hbm.at[e], w1_f32.at[slot], sems.at[slot]),
                pltpu.make_async_copy(w2_hbm.at[e], w2_f32.at[slot], sems.at[slot]))

    @pl.when(i == 0)
    def _():
        slot_ref[0] = 0
        for cp in weight_copies(expert, 0):
            cp.start()

    @pl.when((i < n_used) & new_expert)
    def _():
        slot = slot_ref[0]
        for cp in weight_copies(expert, slot):
            cp.wait()
        i_next = ends_ref[0, expert] // BM

        @pl.when(i_next < n_used)
        def _():
            for cp in weight_copies(blk_ref[0, i_next], 1 - slot):
                cp.start()

        w1_bf[...] = w1_f32[slot].astype(BF16)
        w2_bf[...] = w2_f32[slot].astype(BF16)
        slot_ref[0] = 1 - slot

    @pl.when(i < n_used)
    def _():
        xb = jnp.concatenate(
            [xs_ref[pl.ds(cidx, BM, stride=CHUNKS), :] for cidx in range(CHUNKS)],
            axis=1).astype(BF16)
        h1 = jnp.dot(xb, w1_bf[...], preferred_element_type=F32) + b1_ref[...]
        f = h1.shape[1] // 2
        glu = jnp.minimum(h1[:, :f], SWIGLU_LIMIT)
        lin = jnp.clip(h1[:, f:], -SWIGLU_LIMIT, SWIGLU_LIMIT)
        act = glu * _sigmoid(SWIGLU_ALPHA * glu) * (lin + 1.0)
        y = jnp.dot(act.astype(BF16), w2_bf[...], preferred_element_type=F32) + b2_ref[...]
        for cidx in range(CHUNKS):
            ys_ref[pl.ds(cidx, BM, stride=CHUNKS), :] = y[:, cidx * LANES:(cidx + 1) * LANES]

    @pl.when(i >= n_used)
    def _():
        ys_ref[...] = jnp.zeros_like(ys_ref)


def _experts(blk_e, ends, xs, w1, b1, w2, b2, n_blk):
    E, D, F2 = w1.shape

    def used(i, ends_ref):
        return jnp.minimum(i, ends_ref[0, N_EXPERTS - 1] // BM - 1)

    grid_spec = pltpu.PrefetchScalarGridSpec(
        num_scalar_prefetch=2,
        grid=(n_blk,),
        in_specs=[
            pl.BlockSpec((BM * CHUNKS, LANES), lambda i, blk, ends: (used(i, ends), 0)),
            pl.BlockSpec(memory_space=pl.ANY),
            pl.BlockSpec((None, 1, F2), lambda i, blk, ends: (blk[0, used(i, ends)], 0, 0)),
            pl.BlockSpec(memory_space=pl.ANY),
            pl.BlockSpec((None, 1, D), lambda i, blk, ends: (blk[0, used(i, ends)], 0, 0)),
        ],
        out_specs=pl.BlockSpec((BM * CHUNKS, LANES), lambda i, blk, ends: (i, 0)),
        scratch_shapes=[
            pltpu.VMEM((2, D, F2), F32),
            pltpu.VMEM((2, F2 // 2, D), F32),
            pltpu.VMEM((D, F2), BF16),
            pltpu.VMEM((F2 // 2, D), BF16),
            pltpu.SMEM((1,), jnp.int32),
            pltpu.SemaphoreType.DMA((2,)),
        ],
    )
    return pl.pallas_call(
        _expert_kernel,
        grid_spec=grid_spec,
        out_shape=jax.ShapeDtypeStruct((n_blk * BM * CHUNKS, LANES), F32),
        compiler_params=pltpu.CompilerParams(
            dimension_semantics=("arbitrary",), vmem_limit_bytes=VMEM_LIMIT),
        name="experts",
    )(blk_e, ends, xs, w1, b1, w2, b2)


def _combine_kernel(dest_ref, dest_next_ref, gate_ref, h_ref, gfin_ref, ys_hbm, out_ref,
                    gbuf0, gbuf1, sem0, sem1):
    k = pl.program_id(0)
    tok = COMBINE_TOK

    def issue(d_ref, off, gbuf, sem):
        def body(t, carry):
            for j in range(TOP_K):
                pltpu.make_async_copy(_slab(ys_hbm, d_ref[j, off + t]), _slab(gbuf, j * tok + t),
                                      sem).start(priority=j % 2)
            return carry
        lax.fori_loop(0, tok, body, 0, unroll=2)

    def drain(gbuf, sem):
        pltpu.make_async_copy(ys_hbm.at[pl.ds(0, TOP_K * tok * CHUNKS), :], gbuf, sem).wait()

    def finish(gbuf, off):
        rows = slice(off, off + tok)
        gate = gate_ref[rows, :]
        parts = []
        ss = jnp.zeros((tok, 1), F32)
        for cidx in range(CHUNKS):
            acc = h_ref[rows, cidx * LANES:(cidx + 1) * LANES]
            for j in range(TOP_K):
                acc = acc + gate[:, j:j + 1] * gbuf[pl.ds(j * tok * CHUNKS + cidx, tok,
                                                          stride=CHUNKS), :]
            parts.append(acc)
            ss = ss + jnp.sum(acc * acc, axis=-1, keepdims=True)
        scale = lax.rsqrt(ss * (1.0 / D_MODEL) + RMS_EPS)
        for cidx in range(CHUNKS):
            lanes = slice(cidx * LANES, (cidx + 1) * LANES)
            out_ref[rows, lanes] = parts[cidx] * scale * gfin_ref[:, lanes]

    @pl.when(k == 0)
    def _():
        issue(dest_ref, 0, gbuf0, sem0)

    issue(dest_ref, tok, gbuf1, sem1)
    drain(gbuf0, sem0)
    finish(gbuf0, 0)

    @pl.when(k + 1 < pl.num_programs(0))
    def _():
        issue(dest_next_ref, 0, gbuf0, sem0)

    drain(gbuf1, sem1)
    finish(gbuf1, tok)


def _combine(dest, gate_t, h, gfin, ys):
    T, D = h.shape
    step_tok = 2 * COMBINE_TOK
    n_steps = T // step_tok
    gbuf = pltpu.VMEM((TOP_K * COMBINE_TOK * CHUNKS, LANES), F32)
    return pl.pallas_call(
        _combine_kernel,
        grid=(n_steps,),
        in_specs=[
            pl.BlockSpec((TOP_K, step_tok), lambda i: (0, i), memory_space=pltpu.SMEM),
            pl.BlockSpec((TOP_K, step_tok), lambda i: (0, jnp.minimum(i + 1, n_steps - 1)),
                         memory_space=pltpu.SMEM),
            pl.BlockSpec((step_tok, TOP_K), lambda i: (i, 0)),
            pl.BlockSpec((step_tok, D), lambda i: (i, 0)),
            pl.BlockSpec((1, D), lambda i: (0, 0)),
            pl.BlockSpec(memory_space=pl.ANY),
        ],
        out_specs=pl.BlockSpec((step_tok, D), lambda i: (i, 0)),
        out_shape=jax.ShapeDtypeStruct((T, D), F32),
        scratch_shapes=[gbuf, gbuf, pltpu.SemaphoreType.DMA, pltpu.SemaphoreType.DMA],
        compiler_params=pltpu.CompilerParams(dimension_semantics=("arbitrary",)),
        name="combine",
    )(dest, dest, gate_t, h, gfin, ys)


def kernel(x, positions, g_mix_norm, w_in, w_dw, b_dw, g_conv_ln, b_conv_ln, w_conv_out,
           attn_sinks, w_attn_out, w_out, g_ffn_norm, w_router, b_router,
           w_mlp1, b_mlp1, w_mlp2, b_mlp2, g_final):
    B, S, D = x.shape
    T = B * S
    depth = g_mix_norm.shape[0]
    assert depth == 1 and D == D_MODEL and S % TQ == 0 and T % ROUTE_TOK == 0
    assert T % DISPATCH_TOK == 0 and T % (2 * COMBINE_TOK) == 0
    l = 0
    n_blk = (T * TOP_K) // BM + N_EXPERTS
    n_blk_pad = -(-n_blk // LANES) * LANES

    h, u2s, topi, topg = _mixer(
        x, positions[:, None, :], g_mix_norm[l][None, :],
        w_in[l].astype(BF16), w_in[l][:, COL_Q:COL_G].T.astype(BF16), w_dw[l],
        b_dw[l][None, :], g_conv_ln[l][None, :], b_conv_ln[l][None, :],
        w_conv_out[l].astype(BF16), attn_sinks[l],
        w_attn_out[l].astype(BF16), w_out[l].astype(BF16), g_ffn_norm[l][None, :],
        w_router[l].T, b_router[l][:, None])
    dest, blk_e, ends = _route(topi, n_blk_pad)
    xs = _dispatch(dest, ends, u2s, n_blk * BM)
    ys = _experts(blk_e, ends, xs, w_mlp1[l], b_mlp1[l][:, None, :],
                  w_mlp2[l], b_mlp2[l][:, None, :], n_blk)
    out = _combine(dest, topg.T, h, g_final[None, :], ys)
    return out.reshape(B, S, D)
```
